```python
import math
import jax
import jax.numpy as jnp
from jax import lax
import numpy as np

D_MODEL = 1024
BATCH = 8
SEQ = 4096
DEPTH = 2

GRID_W = 64
CTX_LEN = 256
N_MIXERS = 2
N_GDN_LAYERS = (DEPTH + 1) // 2
N_SWA_LAYERS = DEPTH // 2

GDN_HEADS = 8
GDN_DK = D_MODEL // GDN_HEADS
GDN_DV = GDN_DK
GDN_WIDTH = GDN_HEADS * GDN_DK
GDN_CONV = 5
GDN_CHUNK = 64
GDN_IN_WIDTH = 4 * GDN_WIDTH + 4 * GDN_HEADS

SWA_Q_HEADS = 16
SWA_KV_HEADS = 4
SWA_GROUP = SWA_Q_HEADS // SWA_KV_HEADS
SWA_HEAD_DIM = D_MODEL // SWA_Q_HEADS
SWA_Q_WIDTH = SWA_Q_HEADS * SWA_HEAD_DIM
SWA_KV_WIDTH = SWA_KV_HEADS * SWA_HEAD_DIM
WINDOW = 128
ATTN_BLOCK = 128
ROPE_BASE = 10000.0
ROPE_AXIS_DIM = SWA_HEAD_DIM // 2

PEER_HEADS = 8
N_KEYS = 128
N_EXPERTS = N_KEYS * N_KEYS
PEER_TOPK = 16
PEER_D_KEY = 256
PEER_HALF = PEER_D_KEY // 2
PEER_BLOCK = 128

ADA_CHUNKS = 6
RMS_EPS = 1e-6
NEG_INF = -1e30
F32 = jnp.float32

kernel_name = 'hybrid_gdn_swa_peer_dit'


def rmsnorm(x, g):
    xf = x.astype(F32)
    y = xf * lax.rsqrt(jnp.mean(xf * xf, axis=-1, keepdims=True) + RMS_EPS)
    return (y * g.astype(F32)).astype(x.dtype)


def adaln_params(cvec, w, b):
    m = jax.nn.silu(cvec) @ w + b
    return [mi[..., None, :] for mi in jnp.split(m, ADA_CHUNKS, axis=-1)]


def modulate(x, g, shift, scale):
    return rmsnorm(x, g) * (1 + scale) + shift


def l2norm(x):
    return x * lax.rsqrt(jnp.sum(x * x, axis=-1, keepdims=True) + 1e-6)


def centred_depthwise_conv(x, w):
    k = w.shape[0]
    return lax.conv_general_dilated(x, w[:, None, :].astype(x.dtype), window_strides=(1,),
                                    padding=[(k // 2, k // 2)],
                                    dimension_numbers=('NWC', 'WIO', 'NWC'),
                                    feature_group_count=x.shape[-1])


def gdn_project(h, w_in, conv_w, a_log, dt_bias):
    b, l, _ = h.shape
    p = h @ w_in
    qkv = jax.nn.silu(centred_depthwise_conv(p[..., :3 * GDN_WIDTH], conv_w))
    z = p[..., 3 * GDN_WIDTH:4 * GDN_WIDTH]
    o0 = 4 * GDN_WIDTH
    a = p[..., o0:o0 + 2 * GDN_HEADS].astype(F32).reshape(b, l, 2, GDN_HEADS).transpose(2, 0, 3, 1)
    bt = p[..., o0 + 2 * GDN_HEADS:].astype(F32).reshape(b, l, 2, GDN_HEADS).transpose(2, 0, 3, 1)
    g = -jnp.exp(a_log.astype(F32))[:, None, :, None] * jax.nn.softplus(a + dt_bias.astype(F32)[:, None, :, None])
    beta = jax.nn.sigmoid(bt)

    def heads(t):
        return t.astype(F32).reshape(b, l, GDN_HEADS, -1).transpose(0, 2, 1, 3)

    q = l2norm(heads(qkv[..., :GDN_WIDTH])) * GDN_DK ** -0.5
    k = l2norm(heads(qkv[..., GDN_WIDTH:2 * GDN_WIDTH]))
    v = heads(qkv[..., 2 * GDN_WIDTH:])
    return q, k, v, z, g, beta


def delta_chunked(q, k, v, g, beta, s0, with_output):
    b, h, l, dk = q.shape
    dv = v.shape[-1]
    c = GDN_CHUNK
    n = l // c
    q = q.reshape(b, h, n, c, dk)
    k = k.reshape(b, h, n, c, dk)
    v = v.reshape(b, h, n, c, dv)
    beta = beta.reshape(b, h, n, c)
    big_g = jnp.cumsum(g.reshape(b, h, n, c), axis=-1)
    incl = jnp.tril(jnp.ones((c, c), bool))
    strict = jnp.tril(jnp.ones((c, c), bool), -1)
    diff = big_g[..., :, None] - big_g[..., None, :]
    decay = jnp.where(incl, jnp.exp(jnp.where(incl, diff, 0.0)), 0.0)
    kb = k * beta[..., None]
    lmat = jnp.where(strict, jnp.einsum('bhnid,bhnjd->bhnij', kb, k) * decay, 0.0)
    eye = jnp.eye(c, dtype=F32)
    t_inv = lax.linalg.triangular_solve(lmat + eye, jnp.broadcast_to(eye, lmat.shape),
                                        left_side=True, lower=True, unit_diagonal=True)
    u = jnp.einsum('bhnij,bhnjd->bhnid', t_inv, v * beta[..., None])
    w = jnp.einsum('bhnij,bhnjd->bhnid', t_inv, kb * jnp.exp(big_g)[..., None])
    g_last = big_g[..., -1]
    k_dec = k * jnp.exp(g_last[..., None] - big_g)[..., None]
    chunk_decay = jnp.exp(g_last)
    xs = [k_dec, w, u, chunk_decay]
    if with_output:
        a_qk = jnp.where(incl, jnp.einsum('bhnid,bhnjd->bhnij', q, k) * decay, 0.0)
        q_dec = q * jnp.exp(big_g)[..., None]
        xs = xs + [q_dec, a_qk]
    xs = tuple(jnp.moveaxis(t, 2, 0) for t in xs)

    def step(state, xc):
        kd, wc, uc, cd = xc[:4]
        v_new = uc - jnp.einsum('bhcd,bhde->bhce', wc, state)
        new_state = state * cd[..., None, None] + jnp.einsum('bhcd,bhce->bhde', kd, v_new)
        if with_output:
            qd, aqk = xc[4:]
            o = jnp.einsum('bhcd,bhde->bhce', qd, state) + jnp.einsum('bhcj,bhje->bhce', aqk, v_new)
            return new_state, o
        return new_state, None

    s_final, o = lax.scan(step, s0, xs)
    if not with_output:
        return s_final, None
    return s_final, jnp.moveaxis(o, 0, 2).reshape(b, h, l, dv)


def gdn_bidir(q, k, v, g, beta, s0_f, s0_b, with_output):
    s_f, o_f = delta_chunked(q, k, v, g[0], beta[0], s0_f, with_output)
    fl = lambda t: jnp.flip(t, axis=2)
    s_b, o_b = delta_chunked(fl(q), fl(k), fl(v), jnp.flip(g[1], axis=-1), jnp.flip(beta[1], axis=-1),
                             s0_b, with_output)
    o = o_f + fl(o_b) if with_output else None
    return s_f, s_b, o


def gdn_out(o, z, norm_g, w_out):
    b, h, l, dv = o.shape
    o = o.transpose(0, 2, 1, 3).astype(z.dtype)
    o = rmsnorm(o, norm_g) * jax.nn.silu(z.reshape(b, l, h, dv))
    return o.reshape(b, l, h * dv) @ w_out


def gdn_mixer(h, hc, w_in, conv_w, a_log, dt_bias, norm_g, w_out, ctx_out):
    q, k, v, z, g, beta = gdn_project(h, w_in, conv_w, a_log, dt_bias)
    qc, kc, vc, zc, gc, betac = gdn_project(hc, w_in, conv_w, a_log, dt_bias)
    s0 = jnp.zeros((h.shape[0], GDN_HEADS, GDN_DK, GDN_DV), F32)
    sf_c, sb_c, oc = gdn_bidir(qc, kc, vc, gc, betac, s0, s0, ctx_out)
    _, _, o = gdn_bidir(q, k, v, g, beta, sf_c, sb_c, True)
    y = gdn_out(o, z, norm_g, w_out)
    yc = gdn_out(oc, zc, norm_g, w_out) if ctx_out else None
    return y, yc


def axial_rope_tables(n_lat):
    rows = n_lat // GRID_W
    row = jnp.broadcast_to(jnp.arange(rows)[:, None], (rows, GRID_W)).reshape(-1).astype(F32)
    col = jnp.broadcast_to(jnp.arange(GRID_W)[None, :], (rows, GRID_W)).reshape(-1).astype(F32)
    inv = ROPE_BASE ** (-jnp.arange(0, ROPE_AXIS_DIM, 2, dtype=F32) / ROPE_AXIS_DIM)
    ang_r = row[:, None] * inv
    ang_c = col[:, None] * inv
    return jnp.cos(ang_r), jnp.sin(ang_r), jnp.cos(ang_c), jnp.sin(ang_c)


def rotate(x, cos, sin):
    m = x.shape[-1] // 2
    x1, x2 = x[..., :m], x[..., m:]
    c = cos[:, None, :].astype(x.dtype)
    s = sin[:, None, :].astype(x.dtype)
    return jnp.concatenate([x1 * c - x2 * s, x2 * c + x1 * s], axis=-1)


def axial_rope(x, tables):
    cr, sr, cc, sc = tables
    return jnp.concatenate([rotate(x[..., :ROPE_AXIS_DIM], cr, sr),
                            rotate(x[..., ROPE_AXIS_DIM:], cc, sc)], axis=-1)


def window_attention(q, k, v, kc, vc, sinks):
    b, l = q.shape[0], q.shape[1]
    blk = ATTN_BLOCK
    nb = l // blk
    nw = 3 * blk
    qb = q.reshape(b, nb, blk, SWA_KV_HEADS, SWA_GROUP, SWA_HEAD_DIM)

    def band(t):
        tp = jnp.pad(t, ((0, 0), (blk, blk), (0, 0), (0, 0))).reshape(b, nb + 2, blk, SWA_KV_HEADS, SWA_HEAD_DIM)
        return jnp.concatenate([tp[:, :nb], tp[:, 1:nb + 1], tp[:, 2:]], axis=2)

    kw, vw = band(k), band(v)
    scale = SWA_HEAD_DIM ** -0.5
    s_loc = jnp.einsum('bnqhgd,bnkhd->bnhgqk', qb, kw).astype(F32) * scale
    qpos = jnp.arange(nb)[:, None] * blk + jnp.arange(blk)[None, :]
    kpos = jnp.arange(nb)[:, None] * blk - blk + jnp.arange(nw)[None, :]
    valid = ((jnp.abs(qpos[:, :, None] - kpos[:, None, :]) <= WINDOW)
             & (kpos[:, None, :] >= 0) & (kpos[:, None, :] < l))
    s_loc = jnp.where(valid[None, :, None, None], s_loc, NEG_INF)
    s_ctx = jnp.einsum('bnqhgd,bchd->bnhgqc', qb, kc).astype(F32) * scale
    sink = jnp.broadcast_to(sinks.astype(F32).reshape(SWA_KV_HEADS, SWA_GROUP)[None, None, :, :, None, None],
                            s_loc.shape[:-1] + (1,))
    p = jax.nn.softmax(jnp.concatenate([s_loc, s_ctx, sink], axis=-1), axis=-1)
    p_loc = p[..., :nw].astype(v.dtype)
    p_ctx = p[..., nw:nw + kc.shape[1]].astype(v.dtype)
    o = (jnp.einsum('bnhgqk,bnkhd->bnqhgd', p_loc, vw)
         + jnp.einsum('bnhgqc,bchd->bnqhgd', p_ctx, vc))
    return o.reshape(b, l, SWA_Q_WIDTH)


def context_attention(qc, kc, vc, sinks):
    b, c = qc.shape[0], qc.shape[1]
    qg = qc.reshape(b, c, SWA_KV_HEADS, SWA_GROUP, SWA_HEAD_DIM)
    s = jnp.einsum('bqhgd,bkhd->bhgqk', qg, kc).astype(F32) * SWA_HEAD_DIM ** -0.5
    sink = jnp.broadcast_to(sinks.astype(F32).reshape(SWA_KV_HEADS, SWA_GROUP)[None, :, :, None, None],
                            s.shape[:-1] + (1,))
    p = jax.nn.softmax(jnp.concatenate([s, sink], axis=-1), axis=-1)
    o = jnp.einsum('bhgqk,bkhd->bqhgd', p[..., :c].astype(vc.dtype), vc)
    return o.reshape(b, c, SWA_Q_WIDTH)


def swa_mixer(h, hc, w_in, sinks, w_out, rope_tables, ctx_out):
    b, l, _ = h.shape
    nc = hc.shape[1]
    p = h @ w_in
    q = axial_rope(p[..., :SWA_Q_WIDTH].reshape(b, l, SWA_Q_HEADS, SWA_HEAD_DIM), rope_tables)
    k = axial_rope(p[..., SWA_Q_WIDTH:SWA_Q_WIDTH + SWA_KV_WIDTH].reshape(b, l, SWA_KV_HEADS, SWA_HEAD_DIM),
                   rope_tables)
    v = p[..., SWA_Q_WIDTH + SWA_KV_WIDTH:].reshape(b, l, SWA_KV_HEADS, SWA_HEAD_DIM)
    pc = hc @ (w_in if ctx_out else w_in[:, SWA_Q_WIDTH:])
    kvc = pc[..., -2 * SWA_KV_WIDTH:]
    kc = kvc[..., :SWA_KV_WIDTH].reshape(b, nc, SWA_KV_HEADS, SWA_HEAD_DIM)
    vc = kvc[..., SWA_KV_WIDTH:].reshape(b, nc, SWA_KV_HEADS, SWA_HEAD_DIM)
    y = window_attention(q, k, v, kc, vc, sinks) @ w_out
    yc = None
    if ctx_out:
        qc = pc[..., :SWA_Q_WIDTH].reshape(b, nc, SWA_Q_HEADS, SWA_HEAD_DIM)
        yc = context_attention(qc, kc, vc, sinks) @ w_out
    return y, yc


def peer_ffn(h, w_query, sub_keys, expert_u, expert_v):
    b, l, d = h.shape
    t_count = b * l
    t = h.reshape(t_count, d)
    q = (t @ w_query).reshape(t_count, PEER_HEADS, 2, PEER_HALF)
    s = jnp.einsum('thpd,hpkd->thpk', q, sub_keys).astype(F32)
    s1, i1 = lax.top_k(s[:, :, 0], PEER_TOPK)
    s2, i2 = lax.top_k(s[:, :, 1], PEER_TOPK)
    cand_s = (s1[..., :, None] + s2[..., None, :]).reshape(t_count, PEER_HEADS, PEER_TOPK * PEER_TOPK)
    cand_i = (i1[..., :, None] * N_KEYS + i2[..., None, :]).reshape(t_count, PEER_HEADS, PEER_TOPK * PEER_TOPK)
    top_s, pos = lax.top_k(cand_s, PEER_TOPK)
    idx = jnp.take_along_axis(cand_i, pos, axis=-1).reshape(t_count, PEER_HEADS * PEER_TOPK)
    gate = jax.nn.softmax(top_s, axis=-1).reshape(t_count, PEER_HEADS * PEER_TOPK).astype(h.dtype)
    nblk = t_count // PEER_BLOCK

    def block(args):
        tb, ib, gb = args
        act = jax.nn.gelu(jnp.einsum('pkd,pd->pk', expert_u[ib], tb), approximate=False) * gb
        return jnp.einsum('pk,pkd->pd', act, expert_v[ib])

    out = lax.map(block, (t.reshape(nblk, PEER_BLOCK, d),
                          idx.reshape(nblk, PEER_BLOCK, -1),
                          gate.reshape(nblk, PEER_BLOCK, -1)))
    return out.reshape(b, l, d)


def setup_inputs(seed: int = 0) -> dict:
    key = jax.random.key(seed)
    ks = jax.random.split(key, 22)

    def nrm(k, shape, scale):
        return jax.random.normal(k, shape, jnp.float32) * scale

    dt = jnp.exp(jax.random.uniform(ks[11], (N_GDN_LAYERS, 2, GDN_HEADS), jnp.float32,
                                    math.log(1e-3), math.log(1e-1)))
    return {
        'x': nrm(ks[0], (BATCH, SEQ, D_MODEL), 1.0),
        'c': nrm(ks[1], (BATCH, D_MODEL), 1.0),
        'ctx': nrm(ks[2], (BATCH, CTX_LEN, D_MODEL), 1.0),
        'c_ctx': nrm(ks[3], (D_MODEL,), 1.0),
        'ada_w': nrm(ks[4], (DEPTH, D_MODEL, ADA_CHUNKS * D_MODEL), 0.5 * D_MODEL ** -0.5),
        'ada_b': nrm(ks[5], (DEPTH, ADA_CHUNKS * D_MODEL), 0.02),
        'norm1_g': 1.0 + nrm(ks[6], (DEPTH, D_MODEL), 0.02),
        'norm2_g': 1.0 + nrm(ks[7], (DEPTH, D_MODEL), 0.02),
        'gdn_w_in': nrm(ks[8], (N_GDN_LAYERS, D_MODEL, GDN_IN_WIDTH), D_MODEL ** -0.5),
        'gdn_conv_w': nrm(ks[9], (N_GDN_LAYERS, GDN_CONV, 3 * GDN_WIDTH), GDN_CONV ** -0.5),
        'gdn_a_log': jnp.log(jax.random.uniform(ks[10], (N_GDN_LAYERS, 2, GDN_HEADS), jnp.float32, 1.0, 16.0)),
        'gdn_dt_bias': dt + jnp.log(-jnp.expm1(-dt)),
        'gdn_norm_g': 1.0 + nrm(ks[12], (N_GDN_LAYERS, GDN_DV), 0.02),
        'gdn_w_out': nrm(ks[13], (N_GDN_LAYERS, GDN_WIDTH, D_MODEL), GDN_WIDTH ** -0.5),
        'swa_w_in': nrm(ks[14], (N_SWA_LAYERS, D_MODEL, SWA_Q_WIDTH + 2 * SWA_KV_WIDTH), D_MODEL ** -0.5),
        'swa_sinks': nrm(ks[15], (N_SWA_LAYERS, SWA_Q_HEADS), 0.5),
        'swa_w_out': nrm(ks[16], (N_SWA_LAYERS, SWA_Q_WIDTH, D_MODEL), SWA_Q_WIDTH ** -0.5),
        'peer_w_query': nrm(ks[17], (DEPTH, D_MODEL, PEER_HEADS * PEER_D_KEY), D_MODEL ** -0.5),
        'peer_sub_keys': nrm(ks[18], (DEPTH, PEER_HEADS, 2, N_KEYS, PEER_HALF), PEER_HALF ** -0.5),
        'peer_u': nrm(ks[19], (DEPTH, N_EXPERTS, D_MODEL), D_MODEL ** -0.5),
        'peer_v': nrm(ks[20], (DEPTH, N_EXPERTS, D_MODEL), 0.5),
        'final_g': 1.0 + nrm(ks[21], (D_MODEL,), 0.02),
    }


def reference(x, c, ctx, c_ctx, ada_w, ada_b, norm1_g, norm2_g, gdn_w_in, gdn_conv_w, gdn_a_log,
              gdn_dt_bias, gdn_norm_g, gdn_w_out, swa_w_in, swa_sinks, swa_w_out, peer_w_query,
              peer_sub_keys, peer_u, peer_v, final_g):
    n_lat = x.shape[1]
    rope_tables = axial_rope_tables(n_lat)
    xc = ctx
    for i in range(DEPTH):
        last = i == DEPTH - 1
        sh1, sc1, g1, sh2, sc2, g2 = adaln_params(c, ada_w[i], ada_b[i])
        csh1, csc1, cg1, csh2, csc2, cg2 = adaln_params(c_ctx, ada_w[i], ada_b[i])
        h = modulate(x, norm1_g[i], sh1, sc1)
        hc = modulate(xc, norm1_g[i], csh1, csc1)
        j = i // N_MIXERS
        if i % N_MIXERS == 0:
            y, yc = gdn_mixer(h, hc, gdn_w_in[j], gdn_conv_w[j], gdn_a_log[j], gdn_dt_bias[j],
                              gdn_norm_g[j], gdn_w_out[j], not last)
        else:
            y, yc = swa_mixer(h, hc, swa_w_in[j], swa_sinks[j], swa_w_out[j], rope_tables, not last)
        x = x + g1 * y
        x = x + g2 * peer_ffn(modulate(x, norm2_g[i], sh2, sc2), peer_w_query[i], peer_sub_keys[i],
                              peer_u[i], peer_v[i])
        if not last:
            xc = xc + cg1 * yc
            xc = xc + cg2 * peer_ffn(modulate(xc, norm2_g[i], csh2, csc2), peer_w_query[i],
                                     peer_sub_keys[i], peer_u[i], peer_v[i])
    return rmsnorm(x, final_g)
```

```python
import functools
import math

import numpy as np
import jax
import jax.numpy as jnp
from jax import lax
from jax.experimental import pallas as pl
from jax.experimental.pallas import tpu as pltpu

F32 = jnp.float32
BF16 = jnp.bfloat16

GRID_W = 64
GDN_HEADS = 8
GDN_DK = 128
GDN_CONV = 5
SWA_Q_HEADS = 16
SWA_KV_HEADS = 4
SWA_HEAD_DIM = 64
SWA_GROUP = SWA_Q_HEADS // SWA_KV_HEADS
ROPE_BASE = 10000.0
ROPE_AXIS_DIM = SWA_HEAD_DIM // 2
PEER_HEADS = 8
N_KEYS = 128
PEER_TOPK = 16
ADA_CHUNKS = 6
RMS_EPS = 1e-6
NEG_INF = -1e30

LANES = 128
SUBLANES_BF16 = 16
VMEM_LIMIT_BYTES = 56 * 1024 * 1024

CHUNK = LANES
ATTN_BLOCK = 128
EXPERT_CHUNK = 1024
ADA_TN = 1024
LOWEST = -3.0e38
INV_SQRT2 = 0.7071067811865476


def _cparams(*sem):
    return pltpu.CompilerParams(dimension_semantics=sem, vmem_limit_bytes=VMEM_LIMIT_BYTES)


def _dot(a, b):
    return jnp.dot(a, b, preferred_element_type=F32)


def _dot_nt(a, b):
    return lax.dot_general(a, b, (((1,), (1,)), ((), ())), preferred_element_type=F32)


def _dot_tn(a, b):
    return lax.dot_general(a, b, (((0,), (0,)), ((), ())), preferred_element_type=F32)


def _split2(a):
    hi = a.astype(BF16)
    lo = (a - hi.astype(F32)).astype(BF16)
    return hi, lo


def _split3(a):
    a1 = a.astype(BF16)
    r1 = a - a1.astype(F32)
    a2 = r1.astype(BF16)
    a3 = (r1 - a2.astype(F32)).astype(BF16)
    return a1, a2, a3


def _sigmoid(x):
    return 1.0 / (1.0 + jnp.exp(-x))


def _softplus(x):
    return jnp.maximum(x, 0.0) + jnp.log1p(jnp.exp(-jnp.abs(x)))


def _modulated(x, g, shift, scale):
    ms = jnp.mean(x * x, axis=-1, keepdims=True)
    y = x * lax.rsqrt(ms + RMS_EPS)
    return (y * g) * (1.0 + scale) + shift


def _ada_kernel(c_ref, w_ref, b_ref, o_ref):
    c = c_ref[...]
    s = c * _sigmoid(c)
    sh, sl = _split2(s)
    wh, wl = _split2(w_ref[0])
    o_ref[0] = _dot(sh, wh) + _dot(sh, wl) + _dot(sl, wh) + b_ref[0]


def _ada_params(cmat, ada_w, ada_b):
    depth, d, n = ada_w.shape
    rows = cmat.shape[0]
    return pl.pallas_call(
        _ada_kernel,
        grid=(depth, n // ADA_TN),
        in_specs=[
            pl.BlockSpec((rows, d), lambda l, j: (0, 0)),
            pl.BlockSpec((1, d, ADA_TN), lambda l, j: (l, 0, j)),
            pl.BlockSpec((1, 1, ADA_TN), lambda l, j: (l, 0, j)),
        ],
        out_specs=pl.BlockSpec((1, rows, ADA_TN), lambda l, j: (l, 0, j)),
        out_shape=jax.ShapeDtypeStruct((depth, rows, n), F32),
        compiler_params=_cparams("arbitrary", "arbitrary"),
        name="ada_params",
    )(cmat, ada_w, ada_b.reshape(depth, 1, n))


def _gdn_inproj_kernel(x_ref, g_ref, sh_ref, sc_ref, w_ref, wab_ref, p_ref, ab_ref):
    h = _modulated(x_ref[...], g_ref[...], sh_ref[0], sc_ref[0])
    hb = h.astype(BF16)
    p_ref[...] = _dot(hb, w_ref[...]).astype(BF16)
    hl = (h - hb.astype(F32)).astype(BF16)
    wh, wl = _split2(wab_ref[...])
    ab_ref[...] = _dot_nt(wh, hb) + _dot_nt(wh, hl) + _dot_nt(wl, hb)


def _gdn_inproj(x, seg_fn, tm, norm_g, shift, scale, w_main, w_abt):
    t, d = x.shape
    n = w_main.shape[1]
    na = w_abt.shape[0]
    return pl.pallas_call(
        _gdn_inproj_kernel,
        grid=(t // tm,),
        in_specs=[
            pl.BlockSpec((tm, d), lambda i: (i, 0)),
            pl.BlockSpec((1, d), lambda i: (0, 0)),
            pl.BlockSpec((1, 1, d), lambda i: (seg_fn(i), 0, 0)),
            pl.BlockSpec((1, 1, d), lambda i: (seg_fn(i), 0, 0)),
            pl.BlockSpec((d, n), lambda i: (0, 0)),
            pl.BlockSpec((na, d), lambda i: (0, 0)),
        ],
        out_specs=[
            pl.BlockSpec((tm, n), lambda i: (i, 0)),
            pl.BlockSpec((na, tm), lambda i: (0, i)),
        ],
        out_shape=[jax.ShapeDtypeStruct((t, n), BF16), jax.ShapeDtypeStruct((na, t), F32)],
        compiler_params=_cparams("arbitrary"),
        name="gdn_inproj",
    )(x, norm_g, shift, scale, w_main, w_abt)


CONV_HALO = SUBLANES_BF16
CONV_PAD = GDN_CONV // 2


def _gdn_prep_kernel(main_ref, prev_ref, next_ref, w_ref, q_ref, k_ref, v_ref, ext_scr):
    t = pl.program_id(1)
    nt = pl.num_programs(1)
    tl = main_ref.shape[0]
    base = CONV_HALO
    ext_scr[base:base + tl, :] = main_ref[...].astype(F32)
    prev = prev_ref[...].astype(F32)[CONV_HALO - CONV_PAD:, :]
    nxt = next_ref[...].astype(F32)[:CONV_PAD, :]
    ext_scr[base - CONV_PAD:base, :] = prev * (t > 0).astype(F32)
    ext_scr[base + tl:base + tl + CONV_PAD, :] = nxt * (t < nt - 1).astype(F32)
    nblk = main_ref.shape[1] // LANES
    for cb in range(nblk):
        cols = slice(cb * LANES, (cb + 1) * LANES)
        acc = jnp.zeros((tl, LANES), F32)
        for j in range(GDN_CONV):
            acc = acc + w_ref[j:j + 1, cols] * ext_scr[base - CONV_PAD + j:base - CONV_PAD + j + tl, cols]
        y = acc * _sigmoid(acc)
        which, head = divmod(cb, GDN_HEADS)
        if which < 2:
            y = y * lax.rsqrt(jnp.sum(y * y, axis=-1, keepdims=True) + 1e-6)
            if which == 0:
                y = y * (GDN_DK ** -0.5)
        (q_ref, k_ref, v_ref)[which][head] = y.astype(BF16)


def _gdn_prep(p_main, nseq, seqlen, conv_w8):
    t = p_main.shape[0]
    tl = min(256, seqlen)
    tps = seqlen // tl
    ncol = 3 * GDN_HEADS * GDN_DK
    hb = tl // CONV_HALO
    last_halo = t // CONV_HALO - 1
    out = jax.ShapeDtypeStruct((GDN_HEADS, t, GDN_DK), BF16)
    ospec = pl.BlockSpec((GDN_HEADS, tl, GDN_DK), lambda s, i: (0, s * tps + i, 0))
    return pl.pallas_call(
        _gdn_prep_kernel,
        grid=(nseq, tps),
        in_specs=[
            pl.BlockSpec((tl, ncol), lambda s, i: (s * tps + i, 0)),
            pl.BlockSpec((CONV_HALO, ncol), lambda s, i: (jnp.maximum((s * tps + i) * hb - 1, 0), 0)),
            pl.BlockSpec((CONV_HALO, ncol), lambda s, i: (jnp.minimum((s * tps + i + 1) * hb, last_halo), 0)),
            pl.BlockSpec((8, ncol), lambda s, i: (0, 0)),
        ],
        out_specs=[ospec, ospec, ospec],
        out_shape=[out, out, out],
        scratch_shapes=[pltpu.VMEM((tl + 2 * CONV_HALO, ncol), F32)],
        compiler_params=_cparams("arbitrary", "arbitrary"),
        name="gdn_prep",
    )(p_main, p_main, p_main, conv_w8)


INV_BASE = 16


def _unit_triangular_inverse(lmat, ri, ci):
    c = lmat.shape[0]
    eye = jnp.where(ri == ci, 1.0, 0.0)
    rb = ri // INV_BASE
    cb = ci // INV_BASE
    a = jnp.where(rb == cb, -lmat, 0.0)
    t = eye + a
    ap = a.astype(BF16)
    steps = int(math.log2(INV_BASE))
    for it in range(1, steps):
        ap2 = _dot(ap, ap)
        t = t + _dot(t.astype(BF16), ap2.astype(BF16))
        ap = ap2.astype(BF16)
    size = INV_BASE
    while size < c:
        same_small = (ri // size) == (ci // size)
        same_big = (ri // (2 * size)) == (ci // (2 * size))
        l1 = jnp.where(jnp.logical_and(same_big, jnp.logical_not(same_small)), lmat, 0.0)
        tb = t.astype(BF16)
        t = t - _dot(tb, _dot(l1.astype(BF16), tb).astype(BF16))
        size *= 2
    return t


def _delta_direction(q, k, v, gc_row, beta_row, tot_row, s_prev, lower):
    c = q.shape[0]
    ri = lax.broadcasted_iota(jnp.int32, (c, c), 0)
    ci = lax.broadcasted_iota(jnp.int32, (c, c), 1)
    if lower:
        incl = ri >= ci
        strict = ri > ci
    else:
        incl = ri <= ci
        strict = ri < ci
    gc_rb = jnp.broadcast_to(gc_row, (c, c))
    gc_cb = gc_rb.T
    beta_cb = jnp.broadcast_to(beta_row, (c, c)).T
    tot_b = jnp.broadcast_to(tot_row, (c, c))
    decay = jnp.where(incl, jnp.exp(jnp.where(incl, gc_cb - gc_rb, 0.0)), 0.0)
    kf = k.astype(F32)
    kk = _dot_nt(k, k)
    qk = _dot_nt(q, k)
    lmat = (kk * beta_cb) * jnp.where(strict, decay, 0.0)
    aqk = (qk * decay).astype(BF16)
    eg_cb = jnp.exp(gc_cb)
    rhs = jnp.concatenate([v.astype(F32) * beta_cb, kf * (beta_cb * eg_cb)], axis=1)
    t_inv = _unit_triangular_inverse(lmat, ri, ci)
    x = _dot(t_inv.astype(BF16), rhs.astype(BF16))
    dk = k.shape[1]
    u = x[:, :dk]
    w = x[:, dk:]
    sb = s_prev.astype(BF16)
    v_new = u - _dot(w.astype(BF16), sb)
    vb = v_new.astype(BF16)
    qd = (q.astype(F32) * eg_cb).astype(BF16)
    o = _dot(qd, sb) + _dot(aqk, vb)
    kd = (kf * jnp.exp(tot_b - gc_cb)).astype(BF16)
    s_next = s_prev * jnp.exp(tot_b) + _dot_tn(kd, vb)
    return o, s_next


def _gdn_scan_kernel(qf_ref, kf_ref, vf_ref, qb_ref, kb_ref, vb_ref, abf_ref, abb_ref, alog_ref, dt_ref,
                     s0_ref, of_ref, ob_ref, sfin_ref, s_scr, rows_scr):
    i = pl.program_id(1)
    n = pl.num_programs(1)
    nh = GDN_HEADS

    @pl.when(i == 0)
    def _():
        s_scr[...] = s0_ref[0]

    c = CHUNK
    ri = lax.broadcasted_iota(jnp.int32, (c, c), 0)
    ci = lax.broadcasted_iota(jnp.int32, (c, c), 1)
    one = jnp.ones((c, c), BF16)
    upper_incl = jnp.where(ri <= ci, 1.0, 0.0).astype(BF16)
    lower_incl = jnp.where(ri >= ci, 1.0, 0.0).astype(BF16)

    def cums(g, tri):
        g1, g2, g3 = _split3(g)
        return (_dot(g1, tri) + _dot(g2, tri) + _dot(g3, tri),
                _dot(g1, one) + _dot(g2, one) + _dot(g3, one))

    g_f = -jnp.exp(alog_ref[0:nh, :]) * _softplus(abf_ref[0:nh, :] + dt_ref[0:nh, :])
    g_b = -jnp.exp(alog_ref[nh:2 * nh, :]) * _softplus(abb_ref[nh:2 * nh, :] + dt_ref[nh:2 * nh, :])
    gc_f, tot_f = cums(g_f, upper_incl)
    gc_b, tot_b = cums(g_b, lower_incl)
    rows_scr[0 * nh:1 * nh, :] = gc_f
    rows_scr[1 * nh:2 * nh, :] = gc_b
    rows_scr[2 * nh:3 * nh, :] = _sigmoid(abf_ref[2 * nh:3 * nh, :])
    rows_scr[3 * nh:4 * nh, :] = _sigmoid(abb_ref[3 * nh:4 * nh, :])
    rows_scr[4 * nh:5 * nh, :] = tot_f
    rows_scr[5 * nh:6 * nh, :] = tot_b

    def head_body(h, carry):
        o_f, s_f = _delta_direction(
            qf_ref[h], kf_ref[h], vf_ref[h],
            rows_scr[pl.ds(h, 1), :], rows_scr[pl.ds(2 * nh + h, 1), :], rows_scr[pl.ds(4 * nh + h, 1), :],
            s_scr[0, h], lower=True)
        s_scr[0, h] = s_f
        of_ref[h] = o_f.astype(BF16)
        o_b, s_b = _delta_direction(
            qb_ref[h], kb_ref[h], vb_ref[h],
            rows_scr[pl.ds(nh + h, 1), :], rows_scr[pl.ds(3 * nh + h, 1), :], rows_scr[pl.ds(5 * nh + h, 1), :],
            s_scr[1, h], lower=False)
        s_scr[1, h] = s_b
        ob_ref[h] = o_b.astype(BF16)
        return carry

    lax.fori_loop(0, nh, head_body, 0)

    @pl.when(i == n - 1)
    def _():
        sfin_ref[0] = s_scr[...]


def _gdn_scan(qh, kh, vh, ab_t, nseq, seqlen, alog_rows, dt_rows, s0):
    t = qh.shape[1]
    nt = seqlen // CHUNK
    nab = ab_t.shape[0]
    hspec_f = pl.BlockSpec((GDN_HEADS, CHUNK, GDN_DK), lambda s, i: (0, s * nt + i, 0))
    hspec_b = pl.BlockSpec((GDN_HEADS, CHUNK, GDN_DK), lambda s, i: (0, s * nt + nt - 1 - i, 0))
    sspec = pl.BlockSpec((1, 2, GDN_HEADS, GDN_DK, GDN_DK), lambda s, i: (s, 0, 0, 0, 0))
    oshape = jax.ShapeDtypeStruct((GDN_HEADS, t, GDN_DK), BF16)
    return pl.pallas_call(
        _gdn_scan_kernel,
        grid=(nseq, nt),
        in_specs=[
            hspec_f, hspec_f, hspec_f, hspec_b, hspec_b, hspec_b,
            pl.BlockSpec((nab, CHUNK), lambda s, i: (0, s * nt + i)),
            pl.BlockSpec((nab, CHUNK), lambda s, i: (0, s * nt + nt - 1 - i)),
            pl.BlockSpec((2 * GDN_HEADS, LANES), lambda s, i: (0, 0)),
            pl.BlockSpec((2 * GDN_HEADS, LANES), lambda s, i: (0, 0)),
            sspec,
        ],
        out_specs=[hspec_f, hspec_b, sspec],
        out_shape=[oshape, oshape, jax.ShapeDtypeStruct(s0.shape, F32)],
        scratch_shapes=[
            pltpu.VMEM((2, GDN_HEADS, GDN_DK, GDN_DK), F32),
            pltpu.VMEM((6 * GDN_HEADS, CHUNK), F32),
        ],
        compiler_params=_cparams("arbitrary", "arbitrary"),
        name="gdn_scan",
    )(qh, kh, vh, qh, kh, vh, ab_t, ab_t, alog_rows, dt_rows, s0)


def _gdn_out_kernel(of_ref, ob_ref, z_ref, x_ref, gate_ref, ng_ref, w_ref, o_ref):
    parts = []
    for h in range(GDN_HEADS):
        o = of_ref[h].astype(F32) + ob_ref[h].astype(F32)
        on = o * lax.rsqrt(jnp.mean(o * o, axis=-1, keepdims=True) + RMS_EPS) * ng_ref[...]
        z = z_ref[:, h * GDN_DK:(h + 1) * GDN_DK].astype(F32)
        parts.append((on * (z * _sigmoid(z))).astype(BF16))
    a = jnp.concatenate(parts, axis=-1)
    o_ref[...] = x_ref[...] + gate_ref[0] * _dot(a, w_ref[...])


def _gdn_out(o_f, o_b, p_main, x, seg_fn, tm, gate, norm_g, w_out):
    t, d = x.shape
    width = GDN_HEADS * GDN_DK
    zblk = 3
    hspec = pl.BlockSpec((GDN_HEADS, tm, GDN_DK), lambda i: (0, i, 0))
    return pl.pallas_call(
        _gdn_out_kernel,
        grid=(t // tm,),
        in_specs=[
            hspec, hspec,
            pl.BlockSpec((tm, width), lambda i: (i, zblk)),
            pl.BlockSpec((tm, d), lambda i: (i, 0)),
            pl.BlockSpec((1, 1, d), lambda i: (seg_fn(i), 0, 0)),
            pl.BlockSpec((1, GDN_DK), lambda i: (0, 0)),
            pl.BlockSpec((width, d), lambda i: (0, 0)),
        ],
        out_specs=pl.BlockSpec((tm, d), lambda i: (i, 0)),
        out_shape=jax.ShapeDtypeStruct((t, d), F32),
        compiler_params=_cparams("arbitrary"),
        name="gdn_out",
    )(o_f, o_b, p_main, x, gate, norm_g, w_out)


def _swa_inproj_kernel(x_ref, g_ref, sh_ref, sc_ref, w_ref, cos_ref, sin_ref, q_ref, k_ref, v_ref):
    h = _modulated(x_ref[...], g_ref[...], sh_ref[0], sc_ref[0]).astype(BF16)
    p = _dot(h, w_ref[...])
    qw = SWA_Q_HEADS * SWA_HEAD_DIM
    kw = SWA_KV_HEADS * 2 * SWA_HEAD_DIM
    cos = cos_ref[...]
    sin = sin_ref[...]
    cq = jnp.concatenate([cos] * (qw // LANES), axis=-1)
    sq = jnp.concatenate([sin] * (qw // LANES), axis=-1)
    ck = jnp.concatenate([cos] * (kw // LANES), axis=-1)
    sk = jnp.concatenate([sin] * (kw // LANES), axis=-1)
    o = 0
    q = p[:, o:o + qw] * cq + p[:, o + qw:o + 2 * qw] * sq
    o += 2 * qw
    k = p[:, o:o + kw] * ck + p[:, o + kw:o + 2 * kw] * sk
    o += 2 * kw
    q_ref[...] = (q * (SWA_HEAD_DIM ** -0.5)).astype(BF16)
    k_ref[...] = k.astype(BF16)
    v_ref[...] = p[:, o:o + kw].astype(BF16)


def _swa_inproj(x, seg_fn, tm, norm_g, shift, scale, w_cat, cos_t, sin_t):
    t, d = x.shape
    n = w_cat.shape[1]
    qw = SWA_Q_HEADS * SWA_HEAD_DIM
    kw = SWA_KV_HEADS * 2 * SWA_HEAD_DIM
    tps = cos_t.shape[0] // tm
    return pl.pallas_call(
        _swa_inproj_kernel,
        grid=(t // tm,),
        in_specs=[
            pl.BlockSpec((tm, d), lambda i: (i, 0)),
            pl.BlockSpec((1, d), lambda i: (0, 0)),
            pl.BlockSpec((1, 1, d), lambda i: (seg_fn(i), 0, 0)),
            pl.BlockSpec((1, 1, d), lambda i: (seg_fn(i), 0, 0)),
            pl.BlockSpec((d, n), lambda i: (0, 0)),
            pl.BlockSpec((tm, LANES), lambda i: (i % tps, 0)),
            pl.BlockSpec((tm, LANES), lambda i: (i % tps, 0)),
        ],
        out_specs=[
            pl.BlockSpec((tm, qw), lambda i: (i, 0)),
            pl.BlockSpec((tm, kw), lambda i: (i, 0)),
            pl.BlockSpec((tm, kw), lambda i: (i, 0)),
        ],
        out_shape=[jax.ShapeDtypeStruct((t, qw), BF16), jax.ShapeDtypeStruct((t, kw), BF16),
                   jax.ShapeDtypeStruct((t, kw), BF16)],
        compiler_params=_cparams("arbitrary"),
        name="swa_inproj",
    )(x, norm_g, shift, scale, w_cat, cos_t, sin_t)


def _swa_ctx_kv_kernel(x_ref, g_ref, sh_ref, sc_ref, w_ref, k_ref, v_ref):
    h = _modulated(x_ref[...], g_ref[...], sh_ref[0], sc_ref[0]).astype(BF16)
    p = _dot(h, w_ref[...])
    kw = k_ref.shape[1]
    k_ref[...] = p[:, :kw].astype(BF16)
    v_ref[...] = p[:, kw:].astype(BF16)


def _swa_ctx_kv(xc, seg_fn, tm, norm_g, shift, scale, w_kv):
    t, d = xc.shape
    kw = SWA_KV_HEADS * 2 * SWA_HEAD_DIM
    return pl.pallas_call(
        _swa_ctx_kv_kernel,
        grid=(t // tm,),
        in_specs=[
            pl.BlockSpec((tm, d), lambda i: (i, 0)),
            pl.BlockSpec((1, d), lambda i: (0, 0)),
            pl.BlockSpec((1, 1, d), lambda i: (seg_fn(i), 0, 0)),
            pl.BlockSpec((1, 1, d), lambda i: (seg_fn(i), 0, 0)),
            pl.BlockSpec((d, 2 * kw), lambda i: (0, 0)),
        ],
        out_specs=[pl.BlockSpec((tm, kw), lambda i: (i, 0)), pl.BlockSpec((tm, kw), lambda i: (i, 0))],
        out_shape=[jax.ShapeDtypeStruct((t, kw), BF16), jax.ShapeDtypeStruct((t, kw), BF16)],
        compiler_params=_cparams("arbitrary"),
        name="swa_ctx_kv",
    )(xc, norm_g, shift, scale, w_kv)


def _attn_kernel(q_ref, kp_ref, kc_ref, kn_ref, vp_ref, vc_ref, vn_ref, kx_ref, vx_ref, sink_ref, o_ref):
    n = pl.program_id(1)
    nb = pl.num_programs(1)
    blk = ATTN_BLOCK
    nctx = kx_ref.shape[0]
    ri = lax.broadcasted_iota(jnp.int32, (blk, blk), 0)
    ci = lax.broadcasted_iota(jnp.int32, (blk, blk), 1)
    never = 2 * blk
    ok_prev = (ci - ri) >= jnp.where(n > 0, 0, never)
    ok_next = (ri - ci) >= jnp.where(n < nb - 1, 0, never)
    lane = lax.broadcasted_iota(jnp.int32, (1, LANES), 1)
    lo_half = lane < SWA_HEAD_DIM
    keep_lo = jnp.where(lo_half, 1.0, 0.0).astype(BF16)
    keep_hi = jnp.where(lo_half, 0.0, 1.0).astype(BF16)
    gw = 2 * SWA_HEAD_DIM
    for g in range(SWA_KV_HEADS):
        gs = slice(g * gw, (g + 1) * gw)
        kcat = jnp.concatenate([kp_ref[:, gs], kc_ref[:, gs], kn_ref[:, gs], kx_ref[:, gs]], axis=0)
        vcat = jnp.concatenate([vp_ref[:, gs], vc_ref[:, gs], vn_ref[:, gs], vx_ref[:, gs]], axis=0)
        k_lo = kcat * keep_lo
        k_hi = kcat * keep_hi
        for pr in range(SWA_GROUP // 2):
            col = (g * (SWA_GROUP // 2) + pr) * LANES
            qp = q_ref[:, col:col + LANES]
            outs = []
            for half, km in enumerate((k_lo, k_hi)):
                head = g * SWA_GROUP + pr * 2 + half
                s = _dot_nt(qp, km)
                s = jnp.concatenate([jnp.where(ok_prev, s[:, :blk], NEG_INF), s[:, blk:2 * blk],
                                     jnp.where(ok_next, s[:, 2 * blk:3 * blk], NEG_INF), s[:, 3 * blk:]], axis=1)
                sink = jnp.broadcast_to(sink_ref[head:head + 1, :], (blk, LANES))[:, 0:1]
                m = jnp.maximum(jnp.max(s, axis=-1, keepdims=True), sink)
                p = jnp.exp(s - m)
                den = jnp.sum(p, axis=-1, keepdims=True) + jnp.exp(sink - m)
                outs.append(_dot(p.astype(BF16), vcat) / den)
            o_ref[:, col:col + LANES] = jnp.where(lo_half, outs[0], outs[1]).astype(BF16)


def _attention(q, k2, v2, k2c, v2c, sink_rows, nseq, seqlen, nctx):
    t, qw = q.shape
    kw = k2.shape[1]
    blk = ATTN_BLOCK
    nb = seqlen // blk
    kv_prev = pl.BlockSpec((blk, kw), lambda s, i: (s * nb + jnp.maximum(i - 1, 0), 0))
    kv_cur = pl.BlockSpec((blk, kw), lambda s, i: (s * nb + i, 0))
    kv_next = pl.BlockSpec((blk, kw), lambda s, i: (s * nb + jnp.minimum(i + 1, nb - 1), 0))
    kv_ctx = pl.BlockSpec((nctx, kw), lambda s, i: (s, 0))
    return pl.pallas_call(
        _attn_kernel,
        grid=(nseq, nb),
        in_specs=[
            pl.BlockSpec((blk, qw), lambda s, i: (s * nb + i, 0)),
            kv_prev, kv_cur, kv_next, kv_prev, kv_cur, kv_next, kv_ctx, kv_ctx,
            pl.BlockSpec(sink_rows.shape, lambda s, i: (0, 0)),
        ],
        out_specs=pl.BlockSpec((blk, qw), lambda s, i: (s * nb + i, 0)),
        out_shape=jax.ShapeDtypeStruct((t, qw), BF16),
        compiler_params=_cparams("arbitrary", "arbitrary"),
        name="swa_attention",
    )(q, k2, k2, k2, v2, v2, v2, k2c, v2c, sink_rows)


def _proj_residual_kernel(a_ref, x_ref, gate_ref, w_ref, o_ref):
    o_ref[...] = x_ref[...] + gate_ref[0] * _dot(a_ref[...], w_ref[...])


def _proj_residual(a, x, seg_fn, tm, gate, w):
    t, d = x.shape
    k = a.shape[1]
    return pl.pallas_call(
        _proj_residual_kernel,
        grid=(t // tm,),
        in_specs=[
            pl.BlockSpec((tm, k), lambda i: (i, 0)),
            pl.BlockSpec((tm, d), lambda i: (i, 0)),
            pl.BlockSpec((1, 1, d), lambda i: (seg_fn(i), 0, 0)),
            pl.BlockSpec((k, d), lambda i: (0, 0)),
        ],
        out_specs=pl.BlockSpec((tm, d), lambda i: (i, 0)),
        out_shape=jax.ShapeDtypeStruct((t, d), F32),
        compiler_params=_cparams("arbitrary"),
        name="proj_residual",
    )(a, x, gate, w)


N_RANKS = PEER_TOPK + 1
RANK_ROWS = 24
_CAND_PAIRS = [(r, c) for r in range(N_RANKS) for c in range(N_RANKS) if (r + 1) * (c + 1) <= N_RANKS]


def _tree_max(vals):
    vals = list(vals)
    while len(vals) > 1:
        nxt = [jnp.maximum(vals[i], vals[i + 1]) for i in range(0, len(vals) - 1, 2)]
        if len(vals) % 2:
            nxt.append(vals[-1])
        vals = nxt
    return vals[0]


def _peer_route_kernel(x_ref, g_ref, sh_ref, sc_ref, wqt_ref, keys_ref,
                       h2_ref, s1_ref, s2_ref, rout_ref, qt_scr, a_scr, b_scr):
    tm = x_ref.shape[0]
    nh = PEER_HEADS
    hb = _modulated(x_ref[...], g_ref[...], sh_ref[0], sc_ref[0]).astype(BF16)
    h2_ref[...] = hb
    qt_scr[...] = _dot_nt(wqt_ref[...], hb).astype(BF16)
    nlc = tm // LANES

    def head_body(h, carry):
        for p, (s_ref, top_scr) in enumerate(((s1_ref, a_scr), (s2_ref, b_scr))):
            hp = h * 2 + p
            qs = qt_scr[pl.ds(pl.multiple_of(hp * N_KEYS, N_KEYS), N_KEYS), :]
            s = _dot(keys_ref[hp], qs)
            s_ref[h] = s
            for lc in range(nlc):
                ls = slice(lc * LANES, (lc + 1) * LANES)
                cur = s[:, ls]
                for r in range(N_RANKS):
                    m = jnp.max(cur, axis=0, keepdims=True)
                    top_scr[h, r:r + 1, ls] = m
                    cur = jnp.where(cur == m, LOWEST, cur)
        return carry

    lax.fori_loop(0, nh, head_body, 0)

    for lc in range(nlc):
        ls = slice(lc * LANES, (lc + 1) * LANES)
        av = [jnp.concatenate([a_scr[h, r:r + 1, ls] for h in range(nh)], axis=0) for r in range(N_RANKS)]
        bv = [jnp.concatenate([b_scr[h, r:r + 1, ls] for h in range(nh)], axis=0) for r in range(N_RANKS)]
        cands = [av[r] + bv[c] for (r, c) in _CAND_PAIRS]
        top = cands[0]
        cur = list(cands)
        kth = []
        for it in range(N_RANKS):
            m = _tree_max(cur)
            kth.append(m)
            cur = [jnp.where(cv == m, LOWEST, cv) for cv in cur]
        tau = 0.5 * (kth[PEER_TOPK - 1] + kth[PEER_TOPK])
        z = jnp.zeros_like(top)
        for cv in cands:
            z = z + jnp.where(cv >= tau, jnp.exp(cv - top), 0.0)
        rout_ref[0 * nh:1 * nh, ls] = tau
        rout_ref[1 * nh:2 * nh, ls] = av[0]
        rout_ref[2 * nh:3 * nh, ls] = bv[0]
        rout_ref[3 * nh:4 * nh, ls] = 1.0 / z


def _peer_route(x, seg_fn, tm, norm_g, shift, scale, wq_t, keys):
    t, d = x.shape
    nq = wq_t.shape[0]
    nh = PEER_HEADS
    sshape = jax.ShapeDtypeStruct((nh, N_KEYS, t), F32)
    sspec = pl.BlockSpec((nh, N_KEYS, tm), lambda i: (0, 0, i))
    return pl.pallas_call(
        _peer_route_kernel,
        grid=(t // tm,),
        in_specs=[
            pl.BlockSpec((tm, d), lambda i: (i, 0)),
            pl.BlockSpec((1, d), lambda i: (0, 0)),
            pl.BlockSpec((1, 1, d), lambda i: (seg_fn(i), 0, 0)),
            pl.BlockSpec((1, 1, d), lambda i: (seg_fn(i), 0, 0)),
            pl.BlockSpec((nq, d), lambda i: (0, 0)),
            pl.BlockSpec(keys.shape, lambda i: (0, 0, 0)),
        ],
        out_specs=[
            pl.BlockSpec((tm, d), lambda i: (i, 0)),
            sspec, sspec,
            pl.BlockSpec((4 * nh, tm), lambda i: (0, i)),
        ],
        out_shape=[jax.ShapeDtypeStruct((t, d), BF16), sshape, sshape,
                   jax.ShapeDtypeStruct((4 * nh, t), F32)],
        scratch_shapes=[
            pltpu.VMEM((nq, tm), BF16),
            pltpu.VMEM((nh, RANK_ROWS, tm), F32),
            pltpu.VMEM((nh, RANK_ROWS, tm), F32),
        ],
        compiler_params=_cparams("arbitrary"),
        name="peer_route",
    )(x, norm_g, shift, scale, wq_t, keys)


PEER_LANE_CHUNK = 256


def _peer_expert_kernel(h2_ref, s1_ref, s2_ref, rout_ref, u_ref, vt_ref, x_ref, gate_ref, fg_ref,
                        o_ref, acc_scr, e2_scr, ht_scr, a_scr, *, final_norm):
    c = pl.program_id(1)
    nc = pl.num_programs(1)
    nh = PEER_HEADS
    tm = h2_ref.shape[0]
    per_step = EXPERT_CHUNK // N_KEYS

    @pl.when(c == 0)
    def _():
        acc_scr[...] = jnp.zeros_like(acc_scr)
        for h in range(nh):
            e2_scr[h] = jnp.exp(s2_ref[h] - rout_ref[2 * nh + h:2 * nh + h + 1, :])

    ht_scr[...] = _dot_nt(u_ref[...], h2_ref[...])
    tau = rout_ref[0:nh, :]
    m1 = rout_ref[nh:2 * nh, :]
    inv_z = rout_ref[3 * nh:4 * nh, :]
    lcw = min(PEER_LANE_CHUNK, tm)

    def key_body(ii, carry):
        i = c * per_step + ii
        s1rows = jnp.concatenate([s1_ref[h, pl.ds(i, 1), :] for h in range(nh)], axis=0)
        thr = tau - s1rows
        coef = jnp.exp(s1rows - m1) * inv_z
        row0 = pl.multiple_of(ii * N_KEYS, N_KEYS)
        for lc in range(tm // lcw):
            ls = slice(lc * lcw, (lc + 1) * lcw)
            gate = jnp.zeros((N_KEYS, lcw), F32)
            for h in range(nh):
                sel = jnp.where(s2_ref[h, :, ls] >= thr[h:h + 1, ls], e2_scr[h, :, ls], 0.0)
                gate = gate + sel * coef[h:h + 1, ls]
            hh = ht_scr[pl.ds(row0, N_KEYS), ls]
            act = 0.5 * hh * (1.0 + lax.erf(hh * INV_SQRT2))
            a_scr[pl.ds(row0, N_KEYS), ls] = (act * gate).astype(BF16)
        return carry

    lax.fori_loop(0, per_step, key_body, 0)
    acc_scr[...] += _dot(vt_ref[...], a_scr[...])

    @pl.when(c == nc - 1)
    def _():
        out = x_ref[...] + gate_ref[0] * acc_scr[...].T
        if final_norm:
            ms = jnp.mean(out * out, axis=-1, keepdims=True)
            out = out * lax.rsqrt(ms + RMS_EPS) * fg_ref[...]
        o_ref[...] = out


def _peer_experts(h2, s1, s2, rout, u_bf, vt_bf, x, seg_fn, tm, gate, final_g, final_norm):
    t, d = x.shape
    ne = u_bf.shape[0]
    nh = PEER_HEADS
    sspec = pl.BlockSpec((nh, N_KEYS, tm), lambda i, c: (0, 0, i))
    return pl.pallas_call(
        functools.partial(_peer_expert_kernel, final_norm=final_norm),
        grid=(t // tm, ne // EXPERT_CHUNK),
        in_specs=[
            pl.BlockSpec((tm, d), lambda i, c: (i, 0)),
            sspec, sspec,
            pl.BlockSpec((4 * nh, tm), lambda i, c: (0, i)),
            pl.BlockSpec((EXPERT_CHUNK, d), lambda i, c: (c, 0)),
            pl.BlockSpec((d, EXPERT_CHUNK), lambda i, c: (0, c)),
            pl.BlockSpec((tm, d), lambda i, c: (i, 0)),
            pl.BlockSpec((1, 1, d), lambda i, c: (seg_fn(i), 0, 0)),
            pl.BlockSpec((1, d), lambda i, c: (0, 0)),
        ],
        out_specs=pl.BlockSpec((tm, d), lambda i, c: (i, 0)),
        out_shape=jax.ShapeDtypeStruct((t, d), F32),
        scratch_shapes=[
            pltpu.VMEM((d, tm), F32),
            pltpu.VMEM((nh, N_KEYS, tm), F32),
            pltpu.VMEM((EXPERT_CHUNK, tm), F32),
            pltpu.VMEM((EXPERT_CHUNK, tm), BF16),
        ],
        compiler_params=_cparams("arbitrary", "arbitrary"),
        name="peer_experts",
    )(h2, s1, s2, rout, u_bf, vt_bf, x, gate, final_g)


def _peer(x, seg_fn, seqlen, norm_g, shift, scale, gate, wq_t, keys, u_bf, vt_bf, final_g, final_norm):
    tm_r = min(256, seqlen)
    tm_e = min(512, seqlen)
    h2, s1, s2, rout = _peer_route(x, lambda i: seg_fn(i, tm_r), tm_r, norm_g, shift, scale, wq_t, keys)
    return _peer_experts(h2, s1, s2, rout, u_bf, vt_bf, x, lambda i: seg_fn(i, tm_e), tm_e, gate,
                         final_g, final_norm)


def _rope_tables(n_lat):
    rows = n_lat // GRID_W
    row = jnp.broadcast_to(jnp.arange(rows)[:, None], (rows, GRID_W)).reshape(-1).astype(F32)
    col = jnp.broadcast_to(jnp.arange(GRID_W)[None, :], (rows, GRID_W)).reshape(-1).astype(F32)
    inv = ROPE_BASE ** (-jnp.arange(0, ROPE_AXIS_DIM, 2, dtype=F32) / ROPE_AXIS_DIM)
    ang_r = row[:, None] * inv
    ang_c = col[:, None] * inv
    cr, sr, cc, sc = jnp.cos(ang_r), jnp.sin(ang_r), jnp.cos(ang_c), jnp.sin(ang_c)
    cos64 = jnp.concatenate([cr, cr, cc, cc], axis=-1)
    sin64 = jnp.concatenate([-sr, sr, -sc, sc], axis=-1)
    reps = LANES // SWA_HEAD_DIM
    return jnp.tile(cos64, (1, reps)), jnp.tile(sin64, (1, reps))


def _swa_weight_layout(w_in):
    qw = SWA_Q_HEADS * SWA_HEAD_DIM
    kvw = SWA_KV_HEADS * SWA_HEAD_DIM
    half = ROPE_AXIS_DIM // 2
    q_cols = np.arange(qw)
    q_swap = (q_cols // SWA_HEAD_DIM) * SWA_HEAD_DIM + ((q_cols % SWA_HEAD_DIM) ^ half)
    dup = np.arange(SWA_KV_HEADS * 2 * SWA_HEAD_DIM)
    kv_dup = (dup // (2 * SWA_HEAD_DIM)) * SWA_HEAD_DIM + (dup % SWA_HEAD_DIM)
    kv_dup_swap = (kv_dup // SWA_HEAD_DIM) * SWA_HEAD_DIM + ((kv_dup % SWA_HEAD_DIM) ^ half)
    wq = w_in[:, :qw]
    wk = w_in[:, qw:qw + kvw]
    wv = w_in[:, qw + kvw:]
    w_cat = jnp.concatenate([wq, wq[:, q_swap], wk[:, kv_dup], wk[:, kv_dup_swap], wv[:, kv_dup]], axis=1)
    w_kv = jnp.concatenate([wk[:, kv_dup], wv[:, kv_dup]], axis=1)
    return w_cat.astype(BF16), w_kv.astype(BF16)


def kernel(x, c, ctx, c_ctx, ada_w, ada_b, norm1_g, norm2_g, gdn_w_in, gdn_conv_w, gdn_a_log, gdn_dt_bias,
           gdn_norm_g, gdn_w_out, swa_w_in, swa_sinks, swa_w_out, peer_w_query, peer_sub_keys, peer_u, peer_v,
           final_g):
    return _forward(x, c, ctx, c_ctx, ada_w, ada_b, norm1_g, norm2_g, gdn_w_in, gdn_conv_w, gdn_a_log,
                    gdn_dt_bias, gdn_norm_g, gdn_w_out, swa_w_in, swa_sinks, swa_w_out, peer_w_query,
                    peer_sub_keys, peer_u, peer_v, final_g)[0]


def _forward(x, c, ctx, c_ctx, ada_w, ada_b, norm1_g, norm2_g, gdn_w_in, gdn_conv_w, gdn_a_log, gdn_dt_bias,
             gdn_norm_g, gdn_w_out, swa_w_in, swa_sinks, swa_w_out, peer_w_query, peer_sub_keys, peer_u, peer_v,
             final_g):
    b, l, d = x.shape
    nctx = ctx.shape[1]
    depth = ada_w.shape[0]
    assert depth == 2 and l % 256 == 0 and nctx % 128 == 0 and b + 1 <= 16

    mod_rows = 16
    cmat = jnp.zeros((mod_rows, d), F32).at[:b].set(c).at[b].set(c_ctx)
    ada = _ada_params(cmat, ada_w, ada_b)
    mods = ada.reshape(depth, mod_rows, ADA_CHUNKS, d).transpose(0, 2, 1, 3)[:, :, :, None, :]

    xl = x.reshape(b * l, d)
    xc = ctx.reshape(b * nctx, d)

    def lat_seg(i, tm):
        return i // (l // tm)

    def ctx_seg(i, tm):
        return b

    streams = ((lat_seg, l), (ctx_seg, nctx))

    def peer_weights(li):
        wq_t = peer_w_query[li].T.astype(BF16)
        keys = peer_sub_keys[li].reshape(PEER_HEADS * 2, N_KEYS, -1).astype(BF16)
        return wq_t, keys, peer_u[li].astype(BF16), peer_v[li].T.astype(BF16)

    final_row = final_g.reshape(1, d)

    sh1, sc1, g1, sh2, sc2, g2 = (mods[0, j] for j in range(ADA_CHUNKS))
    n1 = norm1_g[0].reshape(1, d)
    n2 = norm2_g[0].reshape(1, d)
    width = GDN_HEADS * GDN_DK
    w_in = gdn_w_in[0]
    w_main = w_in[:, :4 * width].astype(BF16)
    w_abt = w_in[:, 4 * width:].T
    conv_w8 = jnp.zeros((8, 3 * width), F32).at[:GDN_CONV].set(gdn_conv_w[0])
    alog_rows = jnp.broadcast_to(gdn_a_log[0].reshape(2 * GDN_HEADS, 1), (2 * GDN_HEADS, LANES))
    dt_rows = jnp.broadcast_to(gdn_dt_bias[0].reshape(2 * GDN_HEADS, 1), (2 * GDN_HEADS, LANES))
    gdn_ng = gdn_norm_g[0].reshape(1, GDN_DK)
    w_out0 = gdn_w_out[0].astype(BF16)

    tok = {}
    for name, xs, (seg, slen) in (("ctx", xc, streams[1]), ("lat", xl, streams[0])):
        tm = min(512, slen)
        p_main, ab_t = _gdn_inproj(xs, lambda i: seg(i, tm), tm, n1, sh1, sc1, w_main, w_abt)
        qh, kh, vh = _gdn_prep(p_main, b, slen, conv_w8)
        tok[name] = (xs, seg, slen, tm, p_main, ab_t, qh, kh, vh)

    s0 = jnp.zeros((b, 2, GDN_HEADS, GDN_DK, GDN_DK), F32)
    new_x = {}
    for name in ("ctx", "lat"):
        xs, seg, slen, tm, p_main, ab_t, qh, kh, vh = tok[name]
        o_f, o_b, s0 = _gdn_scan(qh, kh, vh, ab_t, b, slen, alog_rows, dt_rows, s0)
        new_x[name] = _gdn_out(o_f, o_b, p_main, xs, lambda i: seg(i, tm), tm, g1, gdn_ng, w_out0)

    wq_t, keys, u_bf, vt_bf = peer_weights(0)
    xl = _peer(new_x["lat"], lat_seg, l, n2, sh2, sc2, g2, wq_t, keys, u_bf, vt_bf, final_row, False)
    xc = _peer(new_x["ctx"], ctx_seg, nctx, n2, sh2, sc2, g2, wq_t, keys, u_bf, vt_bf, final_row, False)
    streams_seen = dict(x1=new_x["lat"], xc1=new_x["ctx"], x2=xl, xc2=xc)

    sh1, sc1, g1, sh2, sc2, g2 = (mods[1, j] for j in range(ADA_CHUNKS))
    n1 = norm1_g[1].reshape(1, d)
    n2 = norm2_g[1].reshape(1, d)
    w_cat, w_kv = _swa_weight_layout(swa_w_in[0])
    cos_t, sin_t = _rope_tables(l)
    tm = min(512, l)
    q, k2, v2 = _swa_inproj(xl, lambda i: lat_seg(i, tm), tm, n1, sh1, sc1, w_cat, cos_t, sin_t)
    tmc = min(512, nctx)
    k2c, v2c = _swa_ctx_kv(xc, lambda i: ctx_seg(i, tmc), tmc, n1, sh1, sc1, w_kv)
    sink_rows = jnp.broadcast_to(swa_sinks[0].reshape(SWA_Q_HEADS, 1), (SWA_Q_HEADS, LANES))
    att = _attention(q, k2, v2, k2c, v2c, sink_rows, b, l, nctx)
    xl = _proj_residual(att, xl, lambda i: lat_seg(i, tm), tm, g1, swa_w_out[0].astype(BF16))
    streams_seen["x3"] = xl
    wq_t, keys, u_bf, vt_bf = peer_weights(1)
    out = _peer(xl, lat_seg, l, n2, sh2, sc2, g2, wq_t, keys, u_bf, vt_bf, final_row, True)
    return out.reshape(b, l, d), streams_seen
```

```python
import functools
import math

import numpy as np
import jax
import jax.numpy as jnp
from jax import lax
from jax.experimental import pallas as pl
from jax.experimental.pallas import tpu as pltpu

F32 = jnp.float32
BF16 = jnp.bfloat16

GRID_W = 64
GDN_HEADS = 8
GDN_DK = 128
GDN_CONV = 5
SWA_Q_HEADS = 16
SWA_KV_HEADS = 4
SWA_HEAD_DIM = 64
SWA_GROUP = SWA_Q_HEADS // SWA_KV_HEADS
ROPE_BASE = 10000.0
ROPE_AXIS_DIM = SWA_HEAD_DIM // 2
PEER_HEADS = 8
N_KEYS = 128
PEER_TOPK = 16
ADA_CHUNKS = 6
RMS_EPS = 1e-6
NEG_INF = -1e30

LANES = 128
SUBLANES_BF16 = 16
VMEM_LIMIT_BYTES = 56 * 1024 * 1024

CHUNK = LANES
ATTN_BLOCK = 128
ATTN_GROUPS_PER_BATCH = 2
EXPERT_CHUNK = 2048
PEER_PIECE = 256
ADA_TN = 1024
LOWEST = -3.0e38
INV_SQRT2 = 0.7071067811865476


def _cparams(*sem):
    return pltpu.CompilerParams(dimension_semantics=sem, vmem_limit_bytes=VMEM_LIMIT_BYTES)


def _dot(a, b):
    return jnp.dot(a, b, preferred_element_type=F32)


def _dot_nt(a, b):
    return lax.dot_general(a, b, (((1,), (1,)), ((), ())), preferred_element_type=F32)


def _dot_tn(a, b):
    return lax.dot_general(a, b, (((0,), (0,)), ((), ())), preferred_element_type=F32)


def _split2(a):
    hi = a.astype(BF16)
    lo = (a - hi.astype(F32)).astype(BF16)
    return hi, lo


def _split3(a):
    a1 = a.astype(BF16)
    r1 = a - a1.astype(F32)
    a2 = r1.astype(BF16)
    a3 = (r1 - a2.astype(F32)).astype(BF16)
    return a1, a2, a3


def _sigmoid(x):
    return 1.0 / (1.0 + jnp.exp(-x))


def _softplus(x):
    return jnp.maximum(x, 0.0) + jnp.log1p(jnp.exp(-jnp.abs(x)))


def _modulated(x, g, shift, scale):
    ms = jnp.mean(x * x, axis=-1, keepdims=True)
    y = x * lax.rsqrt(ms + RMS_EPS)
    return (y * g) * (1.0 + scale) + shift


def _ada_kernel(c_ref, w_ref, b_ref, o_ref):
    c = c_ref[...]
    s = c * _sigmoid(c)
    sh, sl = _split2(s)
    wh, wl = _split2(w_ref[0])
    o_ref[0] = _dot(sh, wh) + _dot(sh, wl) + _dot(sl, wh) + b_ref[0]


def _ada_params(cmat, ada_w, ada_b):
    depth, d, n = ada_w.shape
    rows = cmat.shape[0]
    return pl.pallas_call(
        _ada_kernel,
        grid=(depth, n // ADA_TN),
        in_specs=[
            pl.BlockSpec((rows, d), lambda l, j: (0, 0)),
            pl.BlockSpec((1, d, ADA_TN), lambda l, j: (l, 0, j)),
            pl.BlockSpec((1, 1, ADA_TN), lambda l, j: (l, 0, j)),
        ],
        out_specs=pl.BlockSpec((1, rows, ADA_TN), lambda l, j: (l, 0, j)),
        out_shape=jax.ShapeDtypeStruct((depth, rows, n), F32),
        compiler_params=_cparams("arbitrary", "arbitrary"),
        name="ada_params",
    )(cmat, ada_w, ada_b.reshape(depth, 1, n))


def _gdn_inproj_kernel(x_ref, g_ref, sh_ref, sc_ref, w_ref, wab_ref, p_ref, ab_ref):
    h = _modulated(x_ref[...], g_ref[...], sh_ref[0], sc_ref[0])
    hb = h.astype(BF16)
    p_ref[...] = _dot(hb, w_ref[...]).astype(BF16)
    hl = (h - hb.astype(F32)).astype(BF16)
    wh, wl = _split2(wab_ref[...])
    ab_ref[...] = _dot_nt(wh, hb) + _dot_nt(wh, hl) + _dot_nt(wl, hb)


def _gdn_inproj(x, seg_fn, tm, norm_g, shift, scale, w_main, w_abt):
    t, d = x.shape
    n = w_main.shape[1]
    na = w_abt.shape[0]
    return pl.pallas_call(
        _gdn_inproj_kernel,
        grid=(t // tm,),
        in_specs=[
            pl.BlockSpec((tm, d), lambda i: (i, 0)),
            pl.BlockSpec((1, d), lambda i: (0, 0)),
            pl.BlockSpec((1, 1, d), lambda i: (seg_fn(i), 0, 0)),
            pl.BlockSpec((1, 1, d), lambda i: (seg_fn(i), 0, 0)),
            pl.BlockSpec((d, n), lambda i: (0, 0)),
            pl.BlockSpec((na, d), lambda i: (0, 0)),
        ],
        out_specs=[
            pl.BlockSpec((tm, n), lambda i: (i, 0)),
            pl.BlockSpec((na, tm), lambda i: (0, i)),
        ],
        out_shape=[jax.ShapeDtypeStruct((t, n), BF16), jax.ShapeDtypeStruct((na, t), F32)],
        compiler_params=_cparams("arbitrary"),
        name="gdn_inproj",
    )(x, norm_g, shift, scale, w_main, w_abt)


CONV_HALO = SUBLANES_BF16
CONV_PAD = GDN_CONV // 2


def _gdn_prep_kernel(main_ref, prev_ref, next_ref, w_ref, q_ref, k_ref, v_ref, ext_scr):
    t = pl.program_id(1)
    nt = pl.num_programs(1)
    tl = main_ref.shape[0]
    base = CONV_HALO
    ext_scr[base:base + tl, :] = main_ref[...].astype(F32)
    prev = prev_ref[...].astype(F32)[CONV_HALO - CONV_PAD:, :]
    nxt = next_ref[...].astype(F32)[:CONV_PAD, :]
    ext_scr[base - CONV_PAD:base, :] = prev * (t > 0).astype(F32)
    ext_scr[base + tl:base + tl + CONV_PAD, :] = nxt * (t < nt - 1).astype(F32)
    nblk = main_ref.shape[1] // LANES
    for cb in range(nblk):
        cols = slice(cb * LANES, (cb + 1) * LANES)
        acc = jnp.zeros((tl, LANES), F32)
        for j in range(GDN_CONV):
            acc = acc + w_ref[j:j + 1, cols] * ext_scr[base - CONV_PAD + j:base - CONV_PAD + j + tl, cols]
        y = acc * _sigmoid(acc)
        which, head = divmod(cb, GDN_HEADS)
        if which < 2:
            y = y * lax.rsqrt(jnp.sum(y * y, axis=-1, keepdims=True) + 1e-6)
            if which == 0:
                y = y * (GDN_DK ** -0.5)
        (q_ref, k_ref, v_ref)[which][head] = y.astype(BF16)


def _gdn_prep(p_main, nseq, seqlen, conv_w8):
    t = p_main.shape[0]
    tl = min(256, seqlen)
    tps = seqlen // tl
    ncol = 3 * GDN_HEADS * GDN_DK
    hb = tl // CONV_HALO
    last_halo = t // CONV_HALO - 1
    out = jax.ShapeDtypeStruct((GDN_HEADS, t, GDN_DK), BF16)
    ospec = pl.BlockSpec((GDN_HEADS, tl, GDN_DK), lambda s, i: (0, s * tps + i, 0))
    return pl.pallas_call(
        _gdn_prep_kernel,
        grid=(nseq, tps),
        in_specs=[
            pl.BlockSpec((tl, ncol), lambda s, i: (s * tps + i, 0)),
            pl.BlockSpec((CONV_HALO, ncol), lambda s, i: (jnp.maximum((s * tps + i) * hb - 1, 0), 0)),
            pl.BlockSpec((CONV_HALO, ncol), lambda s, i: (jnp.minimum((s * tps + i + 1) * hb, last_halo), 0)),
            pl.BlockSpec((8, ncol), lambda s, i: (0, 0)),
        ],
        out_specs=[ospec, ospec, ospec],
        out_shape=[out, out, out],
        scratch_shapes=[pltpu.VMEM((tl + 2 * CONV_HALO, ncol), F32)],
        compiler_params=_cparams("arbitrary", "arbitrary"),
        name="gdn_prep",
    )(p_main, p_main, p_main, conv_w8)


INV_BASE = 16


def _each(fn, *lists):
    return [fn(*args) for args in zip(*lists)]


def _unit_triangular_inverses(lmats, ri, ci):
    c = lmats[0].shape[0]
    eye = jnp.where(ri == ci, 1.0, 0.0)
    same_base = (ri // INV_BASE) == (ci // INV_BASE)
    a = [jnp.where(same_base, -l, 0.0) for l in lmats]
    t = [eye + x for x in a]
    ap = [x.astype(BF16) for x in a]
    for _ in range(1, int(math.log2(INV_BASE))):
        ap = [_dot(x, x).astype(BF16) for x in ap]
        t = _each(lambda tt, x: tt + _dot(tt.astype(BF16), x), t, ap)
    size = INV_BASE
    while size < c:
        same_small = (ri // size) == (ci // size)
        same_big = (ri // (2 * size)) == (ci // (2 * size))
        couple = jnp.logical_and(same_big, jnp.logical_not(same_small))
        l1 = [jnp.where(couple, l, 0.0).astype(BF16) for l in lmats]
        tb = [x.astype(BF16) for x in t]
        mid = _each(lambda l, x: _dot(l, x).astype(BF16), l1, tb)
        t = _each(lambda tt, x, m: tt - _dot(x, m), t, tb, mid)
        size *= 2
    return t


def _delta_chunks(q, k, v, gc_row, beta_row, tot_row, s_prev, lower):
    c = q[0].shape[0]
    dk = k[0].shape[1]
    ri = lax.broadcasted_iota(jnp.int32, (c, c), 0)
    ci = lax.broadcasted_iota(jnp.int32, (c, c), 1)
    incl = [ri >= ci if lo else ri <= ci for lo in lower]
    strict = [ri > ci if lo else ri < ci for lo in lower]
    gc_rb = [jnp.broadcast_to(g, (c, c)) for g in gc_row]
    gc_cb = [g.T for g in gc_rb]
    beta_cb = [jnp.broadcast_to(b, (c, c)).T for b in beta_row]
    tot_b = [jnp.broadcast_to(t, (c, c)) for t in tot_row]
    decay = _each(lambda m, gc, gr: jnp.where(m, jnp.exp(jnp.where(m, gc - gr, 0.0)), 0.0), incl, gc_cb, gc_rb)
    kf = [x.astype(F32) for x in k]
    kk = _each(_dot_nt, k, k)
    qk = _each(_dot_nt, q, k)
    lmat = _each(lambda x, b, m, d: (x * b) * jnp.where(m, d, 0.0), kk, beta_cb, strict, decay)
    aqk = _each(lambda x, d: (x * d).astype(BF16), qk, decay)
    eg_cb = [jnp.exp(g) for g in gc_cb]
    rhs = _each(lambda vv, kx, b, e: jnp.concatenate([vv.astype(F32) * b, kx * (b * e)], axis=1).astype(BF16),
                v, kf, beta_cb, eg_cb)
    t_inv = _unit_triangular_inverses(lmat, ri, ci)
    x = _each(lambda t, r: _dot(t.astype(BF16), r), t_inv, rhs)
    sb = [s.astype(BF16) for s in s_prev]
    v_new = _each(lambda xx, s: xx[:, :dk] - _dot(xx[:, dk:].astype(BF16), s), x, sb)
    vb = [vn.astype(BF16) for vn in v_new]
    qd = _each(lambda qq, e: (qq.astype(F32) * e).astype(BF16), q, eg_cb)
    o = _each(lambda a, s, w, vv: _dot(a, s) + _dot(w, vv), qd, sb, aqk, vb)
    kd = _each(lambda kx, t, g: (kx * jnp.exp(t - g)).astype(BF16), kf, tot_b, gc_cb)
    s_next = _each(lambda s, t, a, vv: s * jnp.exp(t) + _dot_tn(a, vv), s_prev, tot_b, kd, vb)
    return o, s_next


def _gdn_scan_kernel(qf_ref, kf_ref, vf_ref, qb_ref, kb_ref, vb_ref, abf_ref, abb_ref, alog_ref, dt_ref,
                     s0_ref, of_ref, ob_ref, sfin_ref, s_scr):
    i = pl.program_id(1)
    n = pl.num_programs(1)
    nh = GDN_HEADS

    @pl.when(i == 0)
    def _():
        s_scr[...] = s0_ref[0]

    c = CHUNK
    ri = lax.broadcasted_iota(jnp.int32, (c, c), 0)
    ci = lax.broadcasted_iota(jnp.int32, (c, c), 1)
    one = jnp.ones((c, c), BF16)
    upper_incl = jnp.where(ri <= ci, 1.0, 0.0).astype(BF16)
    lower_incl = jnp.where(ri >= ci, 1.0, 0.0).astype(BF16)

    def cums(g, tri):
        g1, g2, g3 = _split3(g)
        return (_dot(g1, tri) + _dot(g2, tri) + _dot(g3, tri),
                _dot(g1, one) + _dot(g2, one) + _dot(g3, one))

    g_f = -jnp.exp(alog_ref[0:nh, :]) * _softplus(abf_ref[0:nh, :] + dt_ref[0:nh, :])
    g_b = -jnp.exp(alog_ref[nh:2 * nh, :]) * _softplus(abb_ref[nh:2 * nh, :] + dt_ref[nh:2 * nh, :])
    gc_f, tot_f = cums(g_f, upper_incl)
    gc_b, tot_b = cums(g_b, lower_incl)
    rows = jnp.concatenate([gc_f, gc_b, _sigmoid(abf_ref[2 * nh:3 * nh, :]), _sigmoid(abb_ref[3 * nh:4 * nh, :]),
                            tot_f, tot_b], axis=0)

    row = lambda r: rows[r:r + 1]
    heads = range(nh)
    o, s_next = _delta_chunks(
        q=[qf_ref[h] for h in heads] + [qb_ref[h] for h in heads],
        k=[kf_ref[h] for h in heads] + [kb_ref[h] for h in heads],
        v=[vf_ref[h] for h in heads] + [vb_ref[h] for h in heads],
        gc_row=[row(h) for h in heads] + [row(nh + h) for h in heads],
        beta_row=[row(2 * nh + h) for h in heads] + [row(3 * nh + h) for h in heads],
        tot_row=[row(4 * nh + h) for h in heads] + [row(5 * nh + h) for h in heads],
        s_prev=[s_scr[0, h] for h in heads] + [s_scr[1, h] for h in heads],
        lower=[True] * nh + [False] * nh)
    for h in heads:
        s_scr[0, h] = s_next[h]
        s_scr[1, h] = s_next[nh + h]
        of_ref[h] = o[h].astype(BF16)
        ob_ref[h] = o[nh + h].astype(BF16)

    @pl.when(i == n - 1)
    def _():
        sfin_ref[0] = s_scr[...]


def _gdn_scan(qh, kh, vh, ab_t, nseq, seqlen, alog_rows, dt_rows, s0):
    t = qh.shape[1]
    nt = seqlen // CHUNK
    nab = ab_t.shape[0]
    hspec_f = pl.BlockSpec((GDN_HEADS, CHUNK, GDN_DK), lambda s, i: (0, s * nt + i, 0))
    hspec_b = pl.BlockSpec((GDN_HEADS, CHUNK, GDN_DK), lambda s, i: (0, s * nt + nt - 1 - i, 0))
    sspec = pl.BlockSpec((1, 2, GDN_HEADS, GDN_DK, GDN_DK), lambda s, i: (s, 0, 0, 0, 0))
    oshape = jax.ShapeDtypeStruct((GDN_HEADS, t, GDN_DK), BF16)
    return pl.pallas_call(
        _gdn_scan_kernel,
        grid=(nseq, nt),
        in_specs=[
            hspec_f, hspec_f, hspec_f, hspec_b, hspec_b, hspec_b,
            pl.BlockSpec((nab, CHUNK), lambda s, i: (0, s * nt + i)),
            pl.BlockSpec((nab, CHUNK), lambda s, i: (0, s * nt + nt - 1 - i)),
            pl.BlockSpec((2 * GDN_HEADS, LANES), lambda s, i: (0, 0)),
            pl.BlockSpec((2 * GDN_HEADS, LANES), lambda s, i: (0, 0)),
            sspec,
        ],
        out_specs=[hspec_f, hspec_b, sspec],
        out_shape=[oshape, oshape, jax.ShapeDtypeStruct(s0.shape, F32)],
        scratch_shapes=[
            pltpu.VMEM((2, GDN_HEADS, GDN_DK, GDN_DK), F32),
        ],
        compiler_params=_cparams("arbitrary", "arbitrary"),
        name="gdn_scan",
    )(qh, kh, vh, qh, kh, vh, ab_t, ab_t, alog_rows, dt_rows, s0)


def _gdn_out_kernel(of_ref, ob_ref, z_ref, x_ref, gate_ref, ng_ref, w_ref, o_ref):
    parts = []
    for h in range(GDN_HEADS):
        o = of_ref[h].astype(F32) + ob_ref[h].astype(F32)
        on = o * lax.rsqrt(jnp.mean(o * o, axis=-1, keepdims=True) + RMS_EPS) * ng_ref[...]
        z = z_ref[:, h * GDN_DK:(h + 1) * GDN_DK].astype(F32)
        parts.append((on * (z * _sigmoid(z))).astype(BF16))
    a = jnp.concatenate(parts, axis=-1)
    o_ref[...] = x_ref[...] + gate_ref[0] * _dot(a, w_ref[...])


def _gdn_out(o_f, o_b, p_main, x, seg_fn, tm, gate, norm_g, w_out):
    t, d = x.shape
    width = GDN_HEADS * GDN_DK
    zblk = 3
    hspec = pl.BlockSpec((GDN_HEADS, tm, GDN_DK), lambda i: (0, i, 0))
    return pl.pallas_call(
        _gdn_out_kernel,
        grid=(t // tm,),
        in_specs=[
            hspec, hspec,
            pl.BlockSpec((tm, width), lambda i: (i, zblk)),
            pl.BlockSpec((tm, d), lambda i: (i, 0)),
            pl.BlockSpec((1, 1, d), lambda i: (seg_fn(i), 0, 0)),
            pl.BlockSpec((1, GDN_DK), lambda i: (0, 0)),
            pl.BlockSpec((width, d), lambda i: (0, 0)),
        ],
        out_specs=pl.BlockSpec((tm, d), lambda i: (i, 0)),
        out_shape=jax.ShapeDtypeStruct((t, d), F32),
        compiler_params=_cparams("arbitrary"),
        name="gdn_out",
    )(o_f, o_b, p_main, x, gate, norm_g, w_out)


def _swa_inproj_kernel(x_ref, g_ref, sh_ref, sc_ref, w_ref, cos_ref, sin_ref, q_ref, k_ref, v_ref):
    h = _modulated(x_ref[...], g_ref[...], sh_ref[0], sc_ref[0]).astype(BF16)
    p = _dot(h, w_ref[...])
    qw = SWA_Q_HEADS * SWA_HEAD_DIM
    kw = SWA_KV_HEADS * 2 * SWA_HEAD_DIM
    cos = cos_ref[...]
    sin = sin_ref[...]
    cq = jnp.concatenate([cos] * (qw // LANES), axis=-1)
    sq = jnp.concatenate([sin] * (qw // LANES), axis=-1)
    ck = jnp.concatenate([cos] * (kw // LANES), axis=-1)
    sk = jnp.concatenate([sin] * (kw // LANES), axis=-1)
    o = 0
    q = p[:, o:o + qw] * cq + p[:, o + qw:o + 2 * qw] * sq
    o += 2 * qw
    k = p[:, o:o + kw] * ck + p[:, o + kw:o + 2 * kw] * sk
    o += 2 * kw
    q_ref[...] = (q * (SWA_HEAD_DIM ** -0.5)).astype(BF16)
    k_ref[...] = k.astype(BF16)
    v_ref[...] = p[:, o:o + kw].astype(BF16)


def _swa_inproj(x, seg_fn, tm, norm_g, shift, scale, w_cat, cos_t, sin_t):
    t, d = x.shape
    n = w_cat.shape[1]
    qw = SWA_Q_HEADS * SWA_HEAD_DIM
    kw = SWA_KV_HEADS * 2 * SWA_HEAD_DIM
    tps = cos_t.shape[0] // tm
    return pl.pallas_call(
        _swa_inproj_kernel,
        grid=(t // tm,),
        in_specs=[
            pl.BlockSpec((tm, d), lambda i: (i, 0)),
            pl.BlockSpec((1, d), lambda i: (0, 0)),
            pl.BlockSpec((1, 1, d), lambda i: (seg_fn(i), 0, 0)),
            pl.BlockSpec((1, 1, d), lambda i: (seg_fn(i), 0, 0)),
            pl.BlockSpec((d, n), lambda i: (0, 0)),
            pl.BlockSpec((tm, LANES), lambda i: (i % tps, 0)),
            pl.BlockSpec((tm, LANES), lambda i: (i % tps, 0)),
        ],
        out_specs=[
            pl.BlockSpec((tm, qw), lambda i: (i, 0)),
            pl.BlockSpec((tm, kw), lambda i: (i, 0)),
            pl.BlockSpec((tm, kw), lambda i: (i, 0)),
        ],
        out_shape=[jax.ShapeDtypeStruct((t, qw), BF16), jax.ShapeDtypeStruct((t, kw), BF16),
                   jax.ShapeDtypeStruct((t, kw), BF16)],
        compiler_params=_cparams("arbitrary"),
        name="swa_inproj",
    )(x, norm_g, shift, scale, w_cat, cos_t, sin_t)


def _swa_ctx_kv_kernel(x_ref, g_ref, sh_ref, sc_ref, w_ref, k_ref, v_ref):
    h = _modulated(x_ref[...], g_ref[...], sh_ref[0], sc_ref[0]).astype(BF16)
    p = _dot(h, w_ref[...])
    kw = k_ref.shape[1]
    k_ref[...] = p[:, :kw].astype(BF16)
    v_ref[...] = p[:, kw:].astype(BF16)


def _swa_ctx_kv(xc, seg_fn, tm, norm_g, shift, scale, w_kv):
    t, d = xc.shape
    kw = SWA_KV_HEADS * 2 * SWA_HEAD_DIM
    return pl.pallas_call(
        _swa_ctx_kv_kernel,
        grid=(t // tm,),
        in_specs=[
            pl.BlockSpec((tm, d), lambda i: (i, 0)),
            pl.BlockSpec((1, d), lambda i: (0, 0)),
            pl.BlockSpec((1, 1, d), lambda i: (seg_fn(i), 0, 0)),
            pl.BlockSpec((1, 1, d), lambda i: (seg_fn(i), 0, 0)),
            pl.BlockSpec((d, 2 * kw), lambda i: (0, 0)),
        ],
        out_specs=[pl.BlockSpec((tm, kw), lambda i: (i, 0)), pl.BlockSpec((tm, kw), lambda i: (i, 0))],
        out_shape=[jax.ShapeDtypeStruct((t, kw), BF16), jax.ShapeDtypeStruct((t, kw), BF16)],
        compiler_params=_cparams("arbitrary"),
        name="swa_ctx_kv",
    )(xc, norm_g, shift, scale, w_kv)


def _attn_kernel(q_ref, kp_ref, kc_ref, kn_ref, vp_ref, vc_ref, vn_ref, kx_ref, vx_ref, sink_ref, o_ref):
    n = pl.program_id(1)
    nb = pl.num_programs(1)
    blk = ATTN_BLOCK
    nctx = kx_ref.shape[0]
    ri = lax.broadcasted_iota(jnp.int32, (blk, blk), 0)
    ci = lax.broadcasted_iota(jnp.int32, (blk, blk), 1)
    never = 2 * blk
    ok_prev = (ci - ri) >= jnp.where(n > 0, 0, never)
    ok_next = (ri - ci) >= jnp.where(n < nb - 1, 0, never)
    lane = lax.broadcasted_iota(jnp.int32, (1, LANES), 1)
    lo_half = lane < SWA_HEAD_DIM
    keep_lo = jnp.where(lo_half, 1.0, 0.0).astype(BF16)
    keep_hi = jnp.where(lo_half, 0.0, 1.0).astype(BF16)
    gw = 2 * SWA_HEAD_DIM
    for g0 in range(0, SWA_KV_HEADS, ATTN_GROUPS_PER_BATCH):
        qps, kms, vcats, sinks, cols = [], [], [], [], []
        for g in range(g0, g0 + ATTN_GROUPS_PER_BATCH):
            gs = slice(g * gw, (g + 1) * gw)
            kcat = jnp.concatenate([kp_ref[:, gs], kc_ref[:, gs], kn_ref[:, gs], kx_ref[:, gs]], axis=0)
            vcat = jnp.concatenate([vp_ref[:, gs], vc_ref[:, gs], vn_ref[:, gs], vx_ref[:, gs]], axis=0)
            k_halves = (kcat * keep_lo, kcat * keep_hi)
            for pr in range(SWA_GROUP // 2):
                col = (g * (SWA_GROUP // 2) + pr) * LANES
                qp = q_ref[:, col:col + LANES]
                for half in range(2):
                    head = g * SWA_GROUP + pr * 2 + half
                    qps.append(qp)
                    kms.append(k_halves[half])
                    vcats.append(vcat)
                    sinks.append(jnp.broadcast_to(sink_ref[head:head + 1, :], (blk, LANES))[:, 0:1])
                    cols.append(col)
        s = _each(_dot_nt, qps, kms)
        s = [jnp.concatenate([jnp.where(ok_prev, x[:, :blk], NEG_INF), x[:, blk:2 * blk],
                              jnp.where(ok_next, x[:, 2 * blk:3 * blk], NEG_INF), x[:, 3 * blk:]], axis=1) for x in s]
        m = _each(lambda x, sk: jnp.maximum(jnp.max(x, axis=-1, keepdims=True), sk), s, sinks)
        p = _each(lambda x, mm: jnp.exp(x - mm), s, m)
        den = _each(lambda x, sk, mm: jnp.sum(x, axis=-1, keepdims=True) + jnp.exp(sk - mm), p, sinks, m)
        o = _each(lambda x, vv, dd: _dot(x.astype(BF16), vv) / dd, p, vcats, den)
        for j in range(0, len(o), 2):
            o_ref[:, cols[j]:cols[j] + LANES] = jnp.where(lo_half, o[j], o[j + 1]).astype(BF16)


def _attention(q, k2, v2, k2c, v2c, sink_rows, nseq, seqlen, nctx):
    t, qw = q.shape
    kw = k2.shape[1]
    blk = ATTN_BLOCK
    nb = seqlen // blk
    kv_prev = pl.BlockSpec((blk, kw), lambda s, i: (s * nb + jnp.maximum(i - 1, 0), 0))
    kv_cur = pl.BlockSpec((blk, kw), lambda s, i: (s * nb + i, 0))
    kv_next = pl.BlockSpec((blk, kw), lambda s, i: (s * nb + jnp.minimum(i + 1, nb - 1), 0))
    kv_ctx = pl.BlockSpec((nctx, kw), lambda s, i: (s, 0))
    return pl.pallas_call(
        _attn_kernel,
        grid=(nseq, nb),
        in_specs=[
            pl.BlockSpec((blk, qw), lambda s, i: (s * nb + i, 0)),
            kv_prev, kv_cur, kv_next, kv_prev, kv_cur, kv_next, kv_ctx, kv_ctx,
            pl.BlockSpec(sink_rows.shape, lambda s, i: (0, 0)),
        ],
        out_specs=pl.BlockSpec((blk, qw), lambda s, i: (s * nb + i, 0)),
        out_shape=jax.ShapeDtypeStruct((t, qw), BF16),
        compiler_params=_cparams("arbitrary", "arbitrary"),
        name="swa_attention",
    )(q, k2, k2, k2, v2, v2, v2, k2c, v2c, sink_rows)


def _proj_residual_kernel(a_ref, x_ref, gate_ref, w_ref, o_ref):
    o_ref[...] = x_ref[...] + gate_ref[0] * _dot(a_ref[...], w_ref[...])


def _proj_residual(a, x, seg_fn, tm, gate, w):
    t, d = x.shape
    k = a.shape[1]
    return pl.pallas_call(
        _proj_residual_kernel,
        grid=(t // tm,),
        in_specs=[
            pl.BlockSpec((tm, k), lambda i: (i, 0)),
            pl.BlockSpec((tm, d), lambda i: (i, 0)),
            pl.BlockSpec((1, 1, d), lambda i: (seg_fn(i), 0, 0)),
            pl.BlockSpec((k, d), lambda i: (0, 0)),
        ],
        out_specs=pl.BlockSpec((tm, d), lambda i: (i, 0)),
        out_shape=jax.ShapeDtypeStruct((t, d), F32),
        compiler_params=_cparams("arbitrary"),
        name="proj_residual",
    )(a, x, gate, w)


N_RANKS = PEER_TOPK + 1
RANK_ROWS = 24
_CAND_PAIRS = [(r, c) for r in range(N_RANKS) for c in range(N_RANKS) if (r + 1) * (c + 1) <= N_RANKS]


def _tree_max(vals):
    vals = list(vals)
    while len(vals) > 1:
        nxt = [jnp.maximum(vals[i], vals[i + 1]) for i in range(0, len(vals) - 1, 2)]
        if len(vals) % 2:
            nxt.append(vals[-1])
        vals = nxt
    return vals[0]


def _peer_route_kernel(x_ref, g_ref, sh_ref, sc_ref, wqt_ref, keys_ref,
                       h2_ref, s1_ref, s2_ref, rout_ref, qt_scr, a_scr, b_scr):
    tm = x_ref.shape[0]
    nh = PEER_HEADS
    hb = _modulated(x_ref[...], g_ref[...], sh_ref[0], sc_ref[0]).astype(BF16)
    h2_ref[...] = hb
    qt_scr[...] = _dot_nt(wqt_ref[...], hb).astype(BF16)
    nlc = tm // LANES

    def head_body(h, carry):
        chains = []
        for p, (s_ref, top_scr) in enumerate(((s1_ref, a_scr), (s2_ref, b_scr))):
            hp = h * 2 + p
            qs = qt_scr[pl.ds(pl.multiple_of(hp * N_KEYS, N_KEYS), N_KEYS), :]
            s = _dot(keys_ref[hp], qs)
            s_ref[h] = s
            for lc in range(nlc):
                ls = slice(lc * LANES, (lc + 1) * LANES)
                chains.append((top_scr, ls, s[:, ls]))
        cur = [ch[2] for ch in chains]
        for r in range(N_RANKS):
            m = [jnp.max(x, axis=0, keepdims=True) for x in cur]
            for (top_scr, ls, _), mv in zip(chains, m):
                top_scr[h, r:r + 1, ls] = mv
            cur = [jnp.where(x == mv, LOWEST, x) for x, mv in zip(cur, m)]
        return carry

    lax.fori_loop(0, nh, head_body, 0)

    for lc in range(nlc):
        ls = slice(lc * LANES, (lc + 1) * LANES)
        av = [jnp.concatenate([a_scr[h, r:r + 1, ls] for h in range(nh)], axis=0) for r in range(N_RANKS)]
        bv = [jnp.concatenate([b_scr[h, r:r + 1, ls] for h in range(nh)], axis=0) for r in range(N_RANKS)]
        cands = [av[r] + bv[c] for (r, c) in _CAND_PAIRS]
        top = cands[0]
        cur = list(cands)
        kth = []
        for it in range(N_RANKS):
            m = _tree_max(cur)
            kth.append(m)
            cur = [jnp.where(cv == m, LOWEST, cv) for cv in cur]
        tau = 0.5 * (kth[PEER_TOPK - 1] + kth[PEER_TOPK])
        z = jnp.zeros_like(top)
        for cv in cands:
            z = z + jnp.where(cv >= tau, jnp.exp(cv - top), 0.0)
        rout_ref[0 * nh:1 * nh, ls] = tau
        rout_ref[1 * nh:2 * nh, ls] = av[0]
        rout_ref[2 * nh:3 * nh, ls] = bv[0]
        rout_ref[3 * nh:4 * nh, ls] = 1.0 / z


def _peer_route(x, seg_fn, tm, norm_g, shift, scale, wq_t, keys):
    t, d = x.shape
    nq = wq_t.shape[0]
    nh = PEER_HEADS
    sshape = jax.ShapeDtypeStruct((nh, N_KEYS, t), F32)
    sspec = pl.BlockSpec((nh, N_KEYS, tm), lambda i: (0, 0, i))
    return pl.pallas_call(
        _peer_route_kernel,
        grid=(t // tm,),
        in_specs=[
            pl.BlockSpec((tm, d), lambda i: (i, 0)),
            pl.BlockSpec((1, d), lambda i: (0, 0)),
            pl.BlockSpec((1, 1, d), lambda i: (seg_fn(i), 0, 0)),
            pl.BlockSpec((1, 1, d), lambda i: (seg_fn(i), 0, 0)),
            pl.BlockSpec((nq, d), lambda i: (0, 0)),
            pl.BlockSpec(keys.shape, lambda i: (0, 0, 0)),
        ],
        out_specs=[
            pl.BlockSpec((tm, d), lambda i: (i, 0)),
            sspec, sspec,
            pl.BlockSpec((4 * nh, tm), lambda i: (0, i)),
        ],
        out_shape=[jax.ShapeDtypeStruct((t, d), BF16), sshape, sshape,
                   jax.ShapeDtypeStruct((4 * nh, t), F32)],
        scratch_shapes=[
            pltpu.VMEM((nq, tm), BF16),
            pltpu.VMEM((nh, RANK_ROWS, tm), F32),
            pltpu.VMEM((nh, RANK_ROWS, tm), F32),
        ],
        compiler_params=_cparams("arbitrary"),
        name="peer_route",
    )(x, norm_g, shift, scale, wq_t, keys)


PEER_LANE_CHUNK = 128


def _peer_expert_kernel(h2_ref, s1_ref, s2_ref, rout_ref, u_ref, vt_ref, x_ref, gate_ref, fg_ref,
                        o_ref, acc_scr, e2_scr, ht0_scr, ht1_scr, a0_scr, a1_scr, *, final_norm):
    ht_scr = (ht0_scr, ht1_scr)
    a_scr = (a0_scr, a1_scr)
    c = pl.program_id(1)
    nc = pl.num_programs(1)
    nh = PEER_HEADS
    tm = h2_ref.shape[0]
    per_step = EXPERT_CHUNK // N_KEYS

    @pl.when(c == 0)
    def _():
        acc_scr[...] = jnp.zeros_like(acc_scr)
        for h in range(nh):
            e2_scr[h] = (jnp.exp(s2_ref[h] - rout_ref[2 * nh + h:2 * nh + h + 1, :])
                         * rout_ref[3 * nh + h:3 * nh + h + 1, :])

    tau = rout_ref[0:nh, :]
    m1 = rout_ref[nh:2 * nh, :]
    m2 = rout_ref[2 * nh:3 * nh, :]
    inv_z = rout_ref[3 * nh:4 * nh, :]
    lcw = min(PEER_LANE_CHUNK, tm)
    n_pieces = u_ref.shape[0]
    keys_per_piece = PEER_PIECE // N_KEYS

    def hidden(p, slot):
        ht_scr[slot][...] = _dot_nt(u_ref[p], h2_ref[...])

    def gate_piece(p, slot):
        for kk in range(keys_per_piece):
            i = c * per_step + p * keys_per_piece + kk
            s1rows = jnp.concatenate([s1_ref[h, pl.ds(i, 1), :] for h in range(nh)], axis=0)
            thr = jnp.exp(tau - s1rows - m2) * inv_z
            coef = jnp.exp(s1rows - m1)
            erows = slice(kk * N_KEYS, (kk + 1) * N_KEYS)
            for lc in range(tm // lcw):
                ls = slice(lc * lcw, (lc + 1) * lcw)
                gate = jnp.zeros((N_KEYS, lcw), F32)
                for h in range(nh):
                    e2 = e2_scr[h, :, ls]
                    gate = gate + jnp.where(e2 >= thr[h:h + 1, ls], e2, 0.0) * coef[h:h + 1, ls]
                hh = ht_scr[slot][erows, ls]
                act = 0.5 * hh * (1.0 + lax.erf(hh * INV_SQRT2))
                a_scr[slot][erows, ls] = (act * gate).astype(BF16)

    def project(p, slot):
        acc_scr[...] += _dot(vt_ref[p], a_scr[slot][...])

    hidden(0, 0)
    hidden(1, 1)
    gate_piece(0, 0)

    def stage_pair(j, carry):
        p = 2 * j + 2
        project(p - 2, 0)
        hidden(p, 0)
        gate_piece(p - 1, 1)
        project(p - 1, 1)
        hidden(p + 1, 1)
        gate_piece(p, 0)
        return carry

    lax.fori_loop(0, (n_pieces - 2) // 2, stage_pair, 0)
    gate_piece(n_pieces - 1, 1)
    project(n_pieces - 2, 0)
    project(n_pieces - 1, 1)

    @pl.when(c == nc - 1)
    def _():
        out = x_ref[...] + gate_ref[0] * acc_scr[...].T
        if final_norm:
            ms = jnp.mean(out * out, axis=-1, keepdims=True)
            out = out * lax.rsqrt(ms + RMS_EPS) * fg_ref[...]
        o_ref[...] = out


def _peer_experts(h2, s1, s2, rout, u_bf, vt_bf, x, seg_fn, tm, gate, final_g, final_norm):
    t, d = x.shape
    ppc = EXPERT_CHUNK // PEER_PIECE
    nh = PEER_HEADS
    sspec = pl.BlockSpec((nh, N_KEYS, tm), lambda i, c: (0, 0, i))
    return pl.pallas_call(
        functools.partial(_peer_expert_kernel, final_norm=final_norm),
        grid=(t // tm, u_bf.shape[0] // ppc),
        in_specs=[
            pl.BlockSpec((tm, d), lambda i, c: (i, 0)),
            sspec, sspec,
            pl.BlockSpec((4 * nh, tm), lambda i, c: (0, i)),
            pl.BlockSpec((ppc, PEER_PIECE, d), lambda i, c: (c, 0, 0)),
            pl.BlockSpec((ppc, d, PEER_PIECE), lambda i, c: (c, 0, 0)),
            pl.BlockSpec((tm, d), lambda i, c: (i, 0)),
            pl.BlockSpec((1, 1, d), lambda i, c: (seg_fn(i), 0, 0)),
            pl.BlockSpec((1, d), lambda i, c: (0, 0)),
        ],
        out_specs=pl.BlockSpec((tm, d), lambda i, c: (i, 0)),
        out_shape=jax.ShapeDtypeStruct((t, d), F32),
        scratch_shapes=[
            pltpu.VMEM((d, tm), F32),
            pltpu.VMEM((nh, N_KEYS, tm), F32),
            pltpu.VMEM((PEER_PIECE, tm), F32),
            pltpu.VMEM((PEER_PIECE, tm), F32),
            pltpu.VMEM((PEER_PIECE, tm), BF16),
            pltpu.VMEM((PEER_PIECE, tm), BF16),
        ],
        compiler_params=_cparams("arbitrary", "arbitrary"),
        name="peer_experts",
    )(h2, s1, s2, rout, u_bf, vt_bf, x, gate, final_g)


def _peer(x, seg_fn, seqlen, norm_g, shift, scale, gate, wq_t, keys, u_bf, vt_bf, final_g, final_norm):
    tm_r = min(256, seqlen)
    tm_e = min(512, seqlen)
    h2, s1, s2, rout = _peer_route(x, lambda i: seg_fn(i, tm_r), tm_r, norm_g, shift, scale, wq_t, keys)
    return _peer_experts(h2, s1, s2, rout, u_bf, vt_bf, x, lambda i: seg_fn(i, tm_e), tm_e, gate,
                         final_g, final_norm)


def _rope_tables(n_lat):
    rows = n_lat // GRID_W
    row = jnp.broadcast_to(jnp.arange(rows)[:, None], (rows, GRID_W)).reshape(-1).astype(F32)
    col = jnp.broadcast_to(jnp.arange(GRID_W)[None, :], (rows, GRID_W)).reshape(-1).astype(F32)
    inv = ROPE_BASE ** (-jnp.arange(0, ROPE_AXIS_DIM, 2, dtype=F32) / ROPE_AXIS_DIM)
    ang_r = row[:, None] * inv
    ang_c = col[:, None] * inv
    cr, sr, cc, sc = jnp.cos(ang_r), jnp.sin(ang_r), jnp.cos(ang_c), jnp.sin(ang_c)
    cos64 = jnp.concatenate([cr, cr, cc, cc], axis=-1)
    sin64 = jnp.concatenate([-sr, sr, -sc, sc], axis=-1)
    reps = LANES // SWA_HEAD_DIM
    return jnp.tile(cos64, (1, reps)), jnp.tile(sin64, (1, reps))


def _swa_weight_layout(w_in):
    qw = SWA_Q_HEADS * SWA_HEAD_DIM
    kvw = SWA_KV_HEADS * SWA_HEAD_DIM
    half = ROPE_AXIS_DIM // 2
    q_cols = np.arange(qw)
    q_swap = (q_cols // SWA_HEAD_DIM) * SWA_HEAD_DIM + ((q_cols % SWA_HEAD_DIM) ^ half)
    dup = np.arange(SWA_KV_HEADS * 2 * SWA_HEAD_DIM)
    kv_dup = (dup // (2 * SWA_HEAD_DIM)) * SWA_HEAD_DIM + (dup % SWA_HEAD_DIM)
    kv_dup_swap = (kv_dup // SWA_HEAD_DIM) * SWA_HEAD_DIM + ((kv_dup % SWA_HEAD_DIM) ^ half)
    wq = w_in[:, :qw]
    wk = w_in[:, qw:qw + kvw]
    wv = w_in[:, qw + kvw:]
    w_cat = jnp.concatenate([wq, wq[:, q_swap], wk[:, kv_dup], wk[:, kv_dup_swap], wv[:, kv_dup]], axis=1)
    w_kv = jnp.concatenate([wk[:, kv_dup], wv[:, kv_dup]], axis=1)
    return w_cat.astype(BF16), w_kv.astype(BF16)


def kernel(x, c, ctx, c_ctx, ada_w, ada_b, norm1_g, norm2_g, gdn_w_in, gdn_conv_w, gdn_a_log, gdn_dt_bias,
           gdn_norm_g, gdn_w_out, swa_w_in, swa_sinks, swa_w_out, peer_w_query, peer_sub_keys, peer_u, peer_v,
           final_g):
    return _forward(x, c, ctx, c_ctx, ada_w, ada_b, norm1_g, norm2_g, gdn_w_in, gdn_conv_w, gdn_a_log,
                    gdn_dt_bias, gdn_norm_g, gdn_w_out, swa_w_in, swa_sinks, swa_w_out, peer_w_query,
                    peer_sub_keys, peer_u, peer_v, final_g)[0]


def _forward(x, c, ctx, c_ctx, ada_w, ada_b, norm1_g, norm2_g, gdn_w_in, gdn_conv_w, gdn_a_log, gdn_dt_bias,
             gdn_norm_g, gdn_w_out, swa_w_in, swa_sinks, swa_w_out, peer_w_query, peer_sub_keys, peer_u, peer_v,
             final_g):
    b, l, d = x.shape
    nctx = ctx.shape[1]
    depth = ada_w.shape[0]
    assert depth == 2 and l % 256 == 0 and nctx % 128 == 0 and b + 1 <= 16

    mod_rows = 16
    cmat = jnp.zeros((mod_rows, d), F32).at[:b].set(c).at[b].set(c_ctx)
    ada = _ada_params(cmat, ada_w, ada_b)
    mods = ada.reshape(depth, mod_rows, ADA_CHUNKS, d).transpose(0, 2, 1, 3)[:, :, :, None, :]

    xl = x.reshape(b * l, d)
    xc = ctx.reshape(b * nctx, d)

    def lat_seg(i, tm):
        return i // (l // tm)

    def ctx_seg(i, tm):
        return b

    streams = ((lat_seg, l), (ctx_seg, nctx))

    def peer_weights(li):
        wq_t = peer_w_query[li].T.astype(BF16)
        keys = peer_sub_keys[li].reshape(PEER_HEADS * 2, N_KEYS, -1).astype(BF16)
        u_p = peer_u[li].astype(BF16).reshape(-1, PEER_PIECE, d)
        vt_p = peer_v[li].astype(BF16).reshape(-1, PEER_PIECE, d).transpose(0, 2, 1)
        return wq_t, keys, u_p, vt_p

    final_row = final_g.reshape(1, d)

    sh1, sc1, g1, sh2, sc2, g2 = (mods[0, j] for j in range(ADA_CHUNKS))
    n1 = norm1_g[0].reshape(1, d)
    n2 = norm2_g[0].reshape(1, d)
    width = GDN_HEADS * GDN_DK
    w_in = gdn_w_in[0]
    w_main = w_in[:, :4 * width].astype(BF16)
    w_abt = w_in[:, 4 * width:].T
    conv_w8 = jnp.zeros((8, 3 * width), F32).at[:GDN_CONV].set(gdn_conv_w[0])
    alog_rows = jnp.broadcast_to(gdn_a_log[0].reshape(2 * GDN_HEADS, 1), (2 * GDN_HEADS, LANES))
    dt_rows = jnp.broadcast_to(gdn_dt_bias[0].reshape(2 * GDN_HEADS, 1), (2 * GDN_HEADS, LANES))
    gdn_ng = gdn_norm_g[0].reshape(1, GDN_DK)
    w_out0 = gdn_w_out[0].astype(BF16)

    tok = {}
    for name, xs, (seg, slen) in (("ctx", xc, streams[1]), ("lat", xl, streams[0])):
        tm = min(512, slen)
        p_main, ab_t = _gdn_inproj(xs, lambda i: seg(i, tm), tm, n1, sh1, sc1, w_main, w_abt)
        qh, kh, vh = _gdn_prep(p_main, b, slen, conv_w8)
        tok[name] = (xs, seg, slen, tm, p_main, ab_t, qh, kh, vh)

    s0 = jnp.zeros((b, 2, GDN_HEADS, GDN_DK, GDN_DK), F32)
    new_x = {}
    for name in ("ctx", "lat"):
        xs, seg, slen, tm, p_main, ab_t, qh, kh, vh = tok[name]
        o_f, o_b, s0 = _gdn_scan(qh, kh, vh, ab_t, b, slen, alog_rows, dt_rows, s0)
        new_x[name] = _gdn_out(o_f, o_b, p_main, xs, lambda i: seg(i, tm), tm, g1, gdn_ng, w_out0)

    wq_t, keys, u_bf, vt_bf = peer_weights(0)
    xl = _peer(new_x["lat"], lat_seg, l, n2, sh2, sc2, g2, wq_t, keys, u_bf, vt_bf, final_row, False)
    xc = _peer(new_x["ctx"], ctx_seg, nctx, n2, sh2, sc2, g2, wq_t, keys, u_bf, vt_bf, final_row, False)
    streams_seen = dict(x1=new_x["lat"], xc1=new_x["ctx"], x2=xl, xc2=xc)

    sh1, sc1, g1, sh2, sc2, g2 = (mods[1, j] for j in range(ADA_CHUNKS))
    n1 = norm1_g[1].reshape(1, d)
    n2 = norm2_g[1].reshape(1, d)
    w_cat, w_kv = _swa_weight_layout(swa_w_in[0])
    cos_t, sin_t = _rope_tables(l)
    tm = min(512, l)
    q, k2, v2 = _swa_inproj(xl, lambda i: lat_seg(i, tm), tm, n1, sh1, sc1, w_cat, cos_t, sin_t)
    tmc = min(512, nctx)
    k2c, v2c = _swa_ctx_kv(xc, lambda i: ctx_seg(i, tmc), tmc, n1, sh1, sc1, w_kv)
    sink_rows = jnp.broadcast_to(swa_sinks[0].reshape(SWA_Q_HEADS, 1), (SWA_Q_HEADS, LANES))
    att = _attention(q, k2, v2, k2c, v2c, sink_rows, b, l, nctx)
    xl = _proj_residual(att, xl, lambda i: lat_seg(i, tm), tm, g1, swa_w_out[0].astype(BF16))
    streams_seen["x3"] = xl
    wq_t, keys, u_bf, vt_bf = peer_weights(1)
    out = _peer(xl, lat_seg, l, n2, sh2, sc2, g2, wq_t, keys, u_bf, vt_bf, final_row, True)
    return out.reshape(b, l, d), streams_seen
```

```python
import functools
import math

import numpy as np
import jax
import jax.numpy as jnp
from jax import lax
from jax.experimental import pallas as pl
from jax.experimental.pallas import tpu as pltpu

F32 = jnp.float32
BF16 = jnp.bfloat16

GRID_W = 64
GDN_HEADS = 8
GDN_DK = 128
GDN_CONV = 5
SWA_Q_HEADS = 16
SWA_KV_HEADS = 4
SWA_HEAD_DIM = 64
SWA_GROUP = SWA_Q_HEADS // SWA_KV_HEADS
ROPE_BASE = 10000.0
ROPE_AXIS_DIM = SWA_HEAD_DIM // 2
PEER_HEADS = 8
N_KEYS = 128
PEER_TOPK = 16
ADA_CHUNKS = 6
RMS_EPS = 1e-6
NEG_INF = -1e30

LANES = 128
SUBLANES_BF16 = 16
VMEM_LIMIT_BYTES = 60 * 1024 * 1024

CHUNK = LANES
ATTN_BLOCK = 128
ATTN_GROUPS_PER_BATCH = 2
EXPERT_CHUNK = 4096
PEER_PIECE = 256
ADA_TN = 1024
LOWEST = -3.0e38
INV_SQRT2 = 0.7071067811865476


def _cparams(*sem):
    return pltpu.CompilerParams(dimension_semantics=sem, vmem_limit_bytes=VMEM_LIMIT_BYTES)


def _dot(a, b):
    return jnp.dot(a, b, preferred_element_type=F32)


def _dot_nt(a, b):
    return lax.dot_general(a, b, (((1,), (1,)), ((), ())), preferred_element_type=F32)


def _dot_tn(a, b):
    return lax.dot_general(a, b, (((0,), (0,)), ((), ())), preferred_element_type=F32)


def _split2(a):
    hi = a.astype(BF16)
    lo = (a - hi.astype(F32)).astype(BF16)
    return hi, lo


def _split3(a):
    a1 = a.astype(BF16)
    r1 = a - a1.astype(F32)
    a2 = r1.astype(BF16)
    a3 = (r1 - a2.astype(F32)).astype(BF16)
    return a1, a2, a3


def _sigmoid(x):
    return 1.0 / (1.0 + jnp.exp(-x))


def _softplus(x):
    return jnp.maximum(x, 0.0) + jnp.log1p(jnp.exp(-jnp.abs(x)))


def _modulated(x, g, shift, scale):
    ms = jnp.mean(x * x, axis=-1, keepdims=True)
    y = x * lax.rsqrt(ms + RMS_EPS)
    return (y * g) * (1.0 + scale) + shift


def _ada_kernel(c_ref, w_ref, b_ref, o_ref):
    c = c_ref[...]
    s = c * _sigmoid(c)
    sh, sl = _split2(s)
    wh, wl = _split2(w_ref[0])
    o_ref[0] = _dot(sh, wh) + _dot(sh, wl) + _dot(sl, wh) + b_ref[0]


def _ada_params(cmat, ada_w, ada_b):
    depth, d, n = ada_w.shape
    rows = cmat.shape[0]
    return pl.pallas_call(
        _ada_kernel,
        grid=(depth, n // ADA_TN),
        in_specs=[
            pl.BlockSpec((rows, d), lambda l, j: (0, 0)),
            pl.BlockSpec((1, d, ADA_TN), lambda l, j: (l, 0, j)),
            pl.BlockSpec((1, 1, ADA_TN), lambda l, j: (l, 0, j)),
        ],
        out_specs=pl.BlockSpec((1, rows, ADA_TN), lambda l, j: (l, 0, j)),
        out_shape=jax.ShapeDtypeStruct((depth, rows, n), F32),
        compiler_params=_cparams("arbitrary", "arbitrary"),
        name="ada_params",
    )(cmat, ada_w, ada_b.reshape(depth, 1, n))


def _gdn_inproj_kernel(x_ref, g_ref, sh_ref, sc_ref, w_ref, wab_ref, p_ref, ab_ref):
    h = _modulated(x_ref[...], g_ref[...], sh_ref[0], sc_ref[0])
    hb = h.astype(BF16)
    p_ref[...] = _dot(hb, w_ref[...]).astype(BF16)
    hl = (h - hb.astype(F32)).astype(BF16)
    wh, wl = _split2(wab_ref[...])
    ab_ref[...] = _dot_nt(wh, hb) + _dot_nt(wh, hl) + _dot_nt(wl, hb)


def _gdn_inproj(x, seg_fn, tm, norm_g, shift, scale, w_main, w_abt):
    t, d = x.shape
    n = w_main.shape[1]
    na = w_abt.shape[0]
    return pl.pallas_call(
        _gdn_inproj_kernel,
        grid=(t // tm,),
        in_specs=[
            pl.BlockSpec((tm, d), lambda i: (i, 0)),
            pl.BlockSpec((1, d), lambda i: (0, 0)),
            pl.BlockSpec((1, 1, d), lambda i: (seg_fn(i), 0, 0)),
            pl.BlockSpec((1, 1, d), lambda i: (seg_fn(i), 0, 0)),
            pl.BlockSpec((d, n), lambda i: (0, 0)),
            pl.BlockSpec((na, d), lambda i: (0, 0)),
        ],
        out_specs=[
            pl.BlockSpec((tm, n), lambda i: (i, 0)),
            pl.BlockSpec((na, tm), lambda i: (0, i)),
        ],
        out_shape=[jax.ShapeDtypeStruct((t, n), BF16), jax.ShapeDtypeStruct((na, t), F32)],
        compiler_params=_cparams("arbitrary"),
        name="gdn_inproj",
    )(x, norm_g, shift, scale, w_main, w_abt)


CONV_HALO = SUBLANES_BF16
CONV_PAD = GDN_CONV // 2


def _gdn_prep_kernel(main_ref, prev_ref, next_ref, w_ref, q_ref, k_ref, v_ref, ext_scr):
    t = pl.program_id(1)
    nt = pl.num_programs(1)
    tl = main_ref.shape[0]
    base = CONV_HALO
    ext_scr[base:base + tl, :] = main_ref[...].astype(F32)
    prev = prev_ref[...].astype(F32)[CONV_HALO - CONV_PAD:, :]
    nxt = next_ref[...].astype(F32)[:CONV_PAD, :]
    ext_scr[base - CONV_PAD:base, :] = prev * (t > 0).astype(F32)
    ext_scr[base + tl:base + tl + CONV_PAD, :] = nxt * (t < nt - 1).astype(F32)
    nblk = main_ref.shape[1] // LANES
    for cb in range(nblk):
        cols = slice(cb * LANES, (cb + 1) * LANES)
        acc = jnp.zeros((tl, LANES), F32)
        for j in range(GDN_CONV):
            acc = acc + w_ref[j:j + 1, cols] * ext_scr[base - CONV_PAD + j:base - CONV_PAD + j + tl, cols]
        y = acc * _sigmoid(acc)
        which, head = divmod(cb, GDN_HEADS)
        if which < 2:
            y = y * lax.rsqrt(jnp.sum(y * y, axis=-1, keepdims=True) + 1e-6)
            if which == 0:
                y = y * (GDN_DK ** -0.5)
        (q_ref, k_ref, v_ref)[which][head] = y.astype(BF16)


def _gdn_prep(p_main, nseq, seqlen, conv_w8):
    t = p_main.shape[0]
    tl = min(256, seqlen)
    tps = seqlen // tl
    ncol = 3 * GDN_HEADS * GDN_DK
    hb = tl // CONV_HALO
    last_halo = t // CONV_HALO - 1
    out = jax.ShapeDtypeStruct((GDN_HEADS, t, GDN_DK), BF16)
    ospec = pl.BlockSpec((GDN_HEADS, tl, GDN_DK), lambda s, i: (0, s * tps + i, 0))
    return pl.pallas_call(
        _gdn_prep_kernel,
        grid=(nseq, tps),
        in_specs=[
            pl.BlockSpec((tl, ncol), lambda s, i: (s * tps + i, 0)),
            pl.BlockSpec((CONV_HALO, ncol), lambda s, i: (jnp.maximum((s * tps + i) * hb - 1, 0), 0)),
            pl.BlockSpec((CONV_HALO, ncol), lambda s, i: (jnp.minimum((s * tps + i + 1) * hb, last_halo), 0)),
            pl.BlockSpec((8, ncol), lambda s, i: (0, 0)),
        ],
        out_specs=[ospec, ospec, ospec],
        out_shape=[out, out, out],
        scratch_shapes=[pltpu.VMEM((tl + 2 * CONV_HALO, ncol), F32)],
        compiler_params=_cparams("arbitrary", "arbitrary"),
        name="gdn_prep",
    )(p_main, p_main, p_main, conv_w8)


INV_BASE = 16


def _each(fn, *lists):
    return [fn(*args) for args in zip(*lists)]


def _unit_triangular_inverses(lmats, ri, ci):
    c = lmats[0].shape[0]
    eye = jnp.where(ri == ci, 1.0, 0.0)
    same_base = (ri // INV_BASE) == (ci // INV_BASE)
    a = [jnp.where(same_base, -l, 0.0) for l in lmats]
    t = [eye + x for x in a]
    ap = [x.astype(BF16) for x in a]
    for _ in range(1, int(math.log2(INV_BASE))):
        ap = [_dot(x, x).astype(BF16) for x in ap]
        t = _each(lambda tt, x: tt + _dot(tt.astype(BF16), x), t, ap)
    size = INV_BASE
    while size < c:
        same_small = (ri // size) == (ci // size)
        same_big = (ri // (2 * size)) == (ci // (2 * size))
        couple = jnp.logical_and(same_big, jnp.logical_not(same_small))
        l1 = [jnp.where(couple, l, 0.0).astype(BF16) for l in lmats]
        tb = [x.astype(BF16) for x in t]
        mid = _each(lambda l, x: _dot(l, x).astype(BF16), l1, tb)
        t = _each(lambda tt, x, m: tt - _dot(x, m), t, tb, mid)
        size *= 2
    return t


def _delta_chunks(q, k, v, gc_row, beta_row, tot_row, s_prev, lower):
    c = q[0].shape[0]
    dk = k[0].shape[1]
    ri = lax.broadcasted_iota(jnp.int32, (c, c), 0)
    ci = lax.broadcasted_iota(jnp.int32, (c, c), 1)
    incl = [ri >= ci if lo else ri <= ci for lo in lower]
    strict = [ri > ci if lo else ri < ci for lo in lower]
    gc_rb = [jnp.broadcast_to(g, (c, c)) for g in gc_row]
    gc_cb = [g.T for g in gc_rb]
    beta_cb = [jnp.broadcast_to(b, (c, c)).T for b in beta_row]
    tot_b = [jnp.broadcast_to(t, (c, c)) for t in tot_row]
    decay = _each(lambda m, gc, gr: jnp.where(m, jnp.exp(jnp.where(m, gc - gr, 0.0)), 0.0), incl, gc_cb, gc_rb)
    kf = [x.astype(F32) for x in k]
    kk = _each(_dot_nt, k, k)
    qk = _each(_dot_nt, q, k)
    lmat = _each(lambda x, b, m, d: (x * b) * jnp.where(m, d, 0.0), kk, beta_cb, strict, decay)
    aqk = _each(lambda x, d: (x * d).astype(BF16), qk, decay)
    eg_cb = [jnp.exp(g) for g in gc_cb]
    rhs = _each(lambda vv, kx, b, e: jnp.concatenate([vv.astype(F32) * b, kx * (b * e)], axis=1).astype(BF16),
                v, kf, beta_cb, eg_cb)
    t_inv = _unit_triangular_inverses(lmat, ri, ci)
    x = _each(lambda t, r: _dot(t.astype(BF16), r), t_inv, rhs)
    sb = [s.astype(BF16) for s in s_prev]
    v_new = _each(lambda xx, s: xx[:, :dk] - _dot(xx[:, dk:].astype(BF16), s), x, sb)
    vb = [vn.astype(BF16) for vn in v_new]
    qd = _each(lambda qq, e: (qq.astype(F32) * e).astype(BF16), q, eg_cb)
    o = _each(lambda a, s, w, vv: _dot(a, s) + _dot(w, vv), qd, sb, aqk, vb)
    kd = _each(lambda kx, t, g: (kx * jnp.exp(t - g)).astype(BF16), kf, tot_b, gc_cb)
    s_next = _each(lambda s, t, a, vv: s * jnp.exp(t) + _dot_tn(a, vv), s_prev, tot_b, kd, vb)
    return o, s_next


def _gdn_scan_kernel(qf_ref, kf_ref, vf_ref, qb_ref, kb_ref, vb_ref, abf_ref, abb_ref, alog_ref, dt_ref,
                     s0_ref, of_ref, ob_ref, sfin_ref, s_scr):
    i = pl.program_id(1)
    n = pl.num_programs(1)
    nh = GDN_HEADS

    @pl.when(i == 0)
    def _():
        s_scr[...] = s0_ref[0]

    c = CHUNK
    ri = lax.broadcasted_iota(jnp.int32, (c, c), 0)
    ci = lax.broadcasted_iota(jnp.int32, (c, c), 1)
    one = jnp.ones((c, c), BF16)
    upper_incl = jnp.where(ri <= ci, 1.0, 0.0).astype(BF16)
    lower_incl = jnp.where(ri >= ci, 1.0, 0.0).astype(BF16)

    def cums(g, tri):
        g1, g2, g3 = _split3(g)
        return (_dot(g1, tri) + _dot(g2, tri) + _dot(g3, tri),
                _dot(g1, one) + _dot(g2, one) + _dot(g3, one))

    g_f = -jnp.exp(alog_ref[0:nh, :]) * _softplus(abf_ref[0:nh, :] + dt_ref[0:nh, :])
    g_b = -jnp.exp(alog_ref[nh:2 * nh, :]) * _softplus(abb_ref[nh:2 * nh, :] + dt_ref[nh:2 * nh, :])
    gc_f, tot_f = cums(g_f, upper_incl)
    gc_b, tot_b = cums(g_b, lower_incl)
    rows = jnp.concatenate([gc_f, gc_b, _sigmoid(abf_ref[2 * nh:3 * nh, :]), _sigmoid(abb_ref[3 * nh:4 * nh, :]),
                            tot_f, tot_b], axis=0)

    row = lambda r: rows[r:r + 1]
    heads = range(nh)
    o, s_next = _delta_chunks(
        q=[qf_ref[h] for h in heads] + [qb_ref[h] for h in heads],
        k=[kf_ref[h] for h in heads] + [kb_ref[h] for h in heads],
        v=[vf_ref[h] for h in heads] + [vb_ref[h] for h in heads],
        gc_row=[row(h) for h in heads] + [row(nh + h) for h in heads],
        beta_row=[row(2 * nh + h) for h in heads] + [row(3 * nh + h) for h in heads],
        tot_row=[row(4 * nh + h) for h in heads] + [row(5 * nh + h) for h in heads],
        s_prev=[s_scr[0, h] for h in heads] + [s_scr[1, h] for h in heads],
        lower=[True] * nh + [False] * nh)
    for h in heads:
        s_scr[0, h] = s_next[h]
        s_scr[1, h] = s_next[nh + h]
        of_ref[h] = o[h].astype(BF16)
        ob_ref[h] = o[nh + h].astype(BF16)

    @pl.when(i == n - 1)
    def _():
        sfin_ref[0] = s_scr[...]


def _gdn_scan(qh, kh, vh, ab_t, nseq, seqlen, alog_rows, dt_rows, s0):
    t = qh.shape[1]
    nt = seqlen // CHUNK
    nab = ab_t.shape[0]
    hspec_f = pl.BlockSpec((GDN_HEADS, CHUNK, GDN_DK), lambda s, i: (0, s * nt + i, 0))
    hspec_b = pl.BlockSpec((GDN_HEADS, CHUNK, GDN_DK), lambda s, i: (0, s * nt + nt - 1 - i, 0))
    sspec = pl.BlockSpec((1, 2, GDN_HEADS, GDN_DK, GDN_DK), lambda s, i: (s, 0, 0, 0, 0))
    oshape = jax.ShapeDtypeStruct((GDN_HEADS, t, GDN_DK), BF16)
    return pl.pallas_call(
        _gdn_scan_kernel,
        grid=(nseq, nt),
        in_specs=[
            hspec_f, hspec_f, hspec_f, hspec_b, hspec_b, hspec_b,
            pl.BlockSpec((nab, CHUNK), lambda s, i: (0, s * nt + i)),
            pl.BlockSpec((nab, CHUNK), lambda s, i: (0, s * nt + nt - 1 - i)),
            pl.BlockSpec((2 * GDN_HEADS, LANES), lambda s, i: (0, 0)),
            pl.BlockSpec((2 * GDN_HEADS, LANES), lambda s, i: (0, 0)),
            sspec,
        ],
        out_specs=[hspec_f, hspec_b, sspec],
        out_shape=[oshape, oshape, jax.ShapeDtypeStruct(s0.shape, F32)],
        scratch_shapes=[
            pltpu.VMEM((2, GDN_HEADS, GDN_DK, GDN_DK), F32),
        ],
        compiler_params=_cparams("arbitrary", "arbitrary"),
        name="gdn_scan",
    )(qh, kh, vh, qh, kh, vh, ab_t, ab_t, alog_rows, dt_rows, s0)


def _gdn_out_kernel(of_ref, ob_ref, z_ref, x_ref, gate_ref, ng_ref, w_ref, o_ref):
    parts = []
    for h in range(GDN_HEADS):
        o = of_ref[h].astype(F32) + ob_ref[h].astype(F32)
        on = o * lax.rsqrt(jnp.mean(o * o, axis=-1, keepdims=True) + RMS_EPS) * ng_ref[...]
        z = z_ref[:, h * GDN_DK:(h + 1) * GDN_DK].astype(F32)
        parts.append((on * (z * _sigmoid(z))).astype(BF16))
    a = jnp.concatenate(parts, axis=-1)
    o_ref[...] = x_ref[...] + gate_ref[0] * _dot(a, w_ref[...])


def _gdn_out(o_f, o_b, p_main, x, seg_fn, tm, gate, norm_g, w_out):
    t, d = x.shape
    width = GDN_HEADS * GDN_DK
    zblk = 3
    hspec = pl.BlockSpec((GDN_HEADS, tm, GDN_DK), lambda i: (0, i, 0))
    return pl.pallas_call(
        _gdn_out_kernel,
        grid=(t // tm,),
        in_specs=[
            hspec, hspec,
            pl.BlockSpec((tm, width), lambda i: (i, zblk)),
            pl.BlockSpec((tm, d), lambda i: (i, 0)),
            pl.BlockSpec((1, 1, d), lambda i: (seg_fn(i), 0, 0)),
            pl.BlockSpec((1, GDN_DK), lambda i: (0, 0)),
            pl.BlockSpec((width, d), lambda i: (0, 0)),
        ],
        out_specs=pl.BlockSpec((tm, d), lambda i: (i, 0)),
        out_shape=jax.ShapeDtypeStruct((t, d), F32),
        compiler_params=_cparams("arbitrary"),
        name="gdn_out",
    )(o_f, o_b, p_main, x, gate, norm_g, w_out)


def _swa_inproj_kernel(x_ref, g_ref, sh_ref, sc_ref, w_ref, cos_ref, sin_ref, q_ref, k_ref, v_ref):
    h = _modulated(x_ref[...], g_ref[...], sh_ref[0], sc_ref[0]).astype(BF16)
    p = _dot(h, w_ref[...])
    qw = SWA_Q_HEADS * SWA_HEAD_DIM
    kw = SWA_KV_HEADS * 2 * SWA_HEAD_DIM
    cos = cos_ref[...]
    sin = sin_ref[...]
    cq = jnp.concatenate([cos] * (qw // LANES), axis=-1)
    sq = jnp.concatenate([sin] * (qw // LANES), axis=-1)
    ck = jnp.concatenate([cos] * (kw // LANES), axis=-1)
    sk = jnp.concatenate([sin] * (kw // LANES), axis=-1)
    o = 0
    q = p[:, o:o + qw] * cq + p[:, o + qw:o + 2 * qw] * sq
    o += 2 * qw
    k = p[:, o:o + kw] * ck + p[:, o + kw:o + 2 * kw] * sk
    o += 2 * kw
    q_ref[...] = (q * (SWA_HEAD_DIM ** -0.5)).astype(BF16)
    k_ref[...] = k.astype(BF16)
    v_ref[...] = p[:, o:o + kw].astype(BF16)


def _swa_inproj(x, seg_fn, tm, norm_g, shift, scale, w_cat, cos_t, sin_t):
    t, d = x.shape
    n = w_cat.shape[1]
    qw = SWA_Q_HEADS * SWA_HEAD_DIM
    kw = SWA_KV_HEADS * 2 * SWA_HEAD_DIM
    tps = cos_t.shape[0] // tm
    return pl.pallas_call(
        _swa_inproj_kernel,
        grid=(t // tm,),
        in_specs=[
            pl.BlockSpec((tm, d), lambda i: (i, 0)),
            pl.BlockSpec((1, d), lambda i: (0, 0)),
            pl.BlockSpec((1, 1, d), lambda i: (seg_fn(i), 0, 0)),
            pl.BlockSpec((1, 1, d), lambda i: (seg_fn(i), 0, 0)),
            pl.BlockSpec((d, n), lambda i: (0, 0)),
            pl.BlockSpec((tm, LANES), lambda i: (i % tps, 0)),
            pl.BlockSpec((tm, LANES), lambda i: (i % tps, 0)),
        ],
        out_specs=[
            pl.BlockSpec((tm, qw), lambda i: (i, 0)),
            pl.BlockSpec((tm, kw), lambda i: (i, 0)),
            pl.BlockSpec((tm, kw), lambda i: (i, 0)),
        ],
        out_shape=[jax.ShapeDtypeStruct((t, qw), BF16), jax.ShapeDtypeStruct((t, kw), BF16),
                   jax.ShapeDtypeStruct((t, kw), BF16)],
        compiler_params=_cparams("arbitrary"),
        name="swa_inproj",
    )(x, norm_g, shift, scale, w_cat, cos_t, sin_t)


def _swa_ctx_kv_kernel(x_ref, g_ref, sh_ref, sc_ref, w_ref, k_ref, v_ref):
    h = _modulated(x_ref[...], g_ref[...], sh_ref[0], sc_ref[0]).astype(BF16)
    p = _dot(h, w_ref[...])
    kw = k_ref.shape[1]
    k_ref[...] = p[:, :kw].astype(BF16)
    v_ref[...] = p[:, kw:].astype(BF16)


def _swa_ctx_kv(xc, seg_fn, tm, norm_g, shift, scale, w_kv):
    t, d = xc.shape
    kw = SWA_KV_HEADS * 2 * SWA_HEAD_DIM
    return pl.pallas_call(
        _swa_ctx_kv_kernel,
        grid=(t // tm,),
        in_specs=[
            pl.BlockSpec((tm, d), lambda i: (i, 0)),
            pl.BlockSpec((1, d), lambda i: (0, 0)),
            pl.BlockSpec((1, 1, d), lambda i: (seg_fn(i), 0, 0)),
            pl.BlockSpec((1, 1, d), lambda i: (seg_fn(i), 0, 0)),
            pl.BlockSpec((d, 2 * kw), lambda i: (0, 0)),
        ],
        out_specs=[pl.BlockSpec((tm, kw), lambda i: (i, 0)), pl.BlockSpec((tm, kw), lambda i: (i, 0))],
        out_shape=[jax.ShapeDtypeStruct((t, kw), BF16), jax.ShapeDtypeStruct((t, kw), BF16)],
        compiler_params=_cparams("arbitrary"),
        name="swa_ctx_kv",
    )(xc, norm_g, shift, scale, w_kv)


def _attn_kernel(q_ref, kp_ref, kc_ref, kn_ref, vp_ref, vc_ref, vn_ref, kx_ref, vx_ref, sink_ref, o_ref):
    n = pl.program_id(1)
    nb = pl.num_programs(1)
    blk = ATTN_BLOCK
    nctx = kx_ref.shape[0]
    ri = lax.broadcasted_iota(jnp.int32, (blk, blk), 0)
    ci = lax.broadcasted_iota(jnp.int32, (blk, blk), 1)
    never = 2 * blk
    ok_prev = (ci - ri) >= jnp.where(n > 0, 0, never)
    ok_next = (ri - ci) >= jnp.where(n < nb - 1, 0, never)
    lane = lax.broadcasted_iota(jnp.int32, (1, LANES), 1)
    lo_half = lane < SWA_HEAD_DIM
    keep_lo = jnp.where(lo_half, 1.0, 0.0).astype(BF16)
    keep_hi = jnp.where(lo_half, 0.0, 1.0).astype(BF16)
    gw = 2 * SWA_HEAD_DIM
    for g0 in range(0, SWA_KV_HEADS, ATTN_GROUPS_PER_BATCH):
        qps, kms, vcats, sinks, cols = [], [], [], [], []
        for g in range(g0, g0 + ATTN_GROUPS_PER_BATCH):
            gs = slice(g * gw, (g + 1) * gw)
            kcat = jnp.concatenate([kp_ref[:, gs], kc_ref[:, gs], kn_ref[:, gs], kx_ref[:, gs]], axis=0)
            vcat = jnp.concatenate([vp_ref[:, gs], vc_ref[:, gs], vn_ref[:, gs], vx_ref[:, gs]], axis=0)
            k_halves = (kcat * keep_lo, kcat * keep_hi)
            for pr in range(SWA_GROUP // 2):
                col = (g * (SWA_GROUP // 2) + pr) * LANES
                qp = q_ref[:, col:col + LANES]
                for half in range(2):
                    head = g * SWA_GROUP + pr * 2 + half
                    qps.append(qp)
                    kms.append(k_halves[half])
                    vcats.append(vcat)
                    sinks.append(jnp.broadcast_to(sink_ref[head:head + 1, :], (blk, LANES))[:, 0:1])
                    cols.append(col)
        s = _each(_dot_nt, qps, kms)
        s = [jnp.concatenate([jnp.where(ok_prev, x[:, :blk], NEG_INF), x[:, blk:2 * blk],
                              jnp.where(ok_next, x[:, 2 * blk:3 * blk], NEG_INF), x[:, 3 * blk:]], axis=1) for x in s]
        m = _each(lambda x, sk: jnp.maximum(jnp.max(x, axis=-1, keepdims=True), sk), s, sinks)
        p = _each(lambda x, mm: jnp.exp(x - mm), s, m)
        den = _each(lambda x, sk, mm: jnp.sum(x, axis=-1, keepdims=True) + jnp.exp(sk - mm), p, sinks, m)
        o = _each(lambda x, vv, dd: _dot(x.astype(BF16), vv) / dd, p, vcats, den)
        for j in range(0, len(o), 2):
            o_ref[:, cols[j]:cols[j] + LANES] = jnp.where(lo_half, o[j], o[j + 1]).astype(BF16)


def _attention(q, k2, v2, k2c, v2c, sink_rows, nseq, seqlen, nctx):
    t, qw = q.shape
    kw = k2.shape[1]
    blk = ATTN_BLOCK
    nb = seqlen // blk
    kv_prev = pl.BlockSpec((blk, kw), lambda s, i: (s * nb + jnp.maximum(i - 1, 0), 0))
    kv_cur = pl.BlockSpec((blk, kw), lambda s, i: (s * nb + i, 0))
    kv_next = pl.BlockSpec((blk, kw), lambda s, i: (s * nb + jnp.minimum(i + 1, nb - 1), 0))
    kv_ctx = pl.BlockSpec((nctx, kw), lambda s, i: (s, 0))
    return pl.pallas_call(
        _attn_kernel,
        grid=(nseq, nb),
        in_specs=[
            pl.BlockSpec((blk, qw), lambda s, i: (s * nb + i, 0)),
            kv_prev, kv_cur, kv_next, kv_prev, kv_cur, kv_next, kv_ctx, kv_ctx,
            pl.BlockSpec(sink_rows.shape, lambda s, i: (0, 0)),
        ],
        out_specs=pl.BlockSpec((blk, qw), lambda s, i: (s * nb + i, 0)),
        out_shape=jax.ShapeDtypeStruct((t, qw), BF16),
        compiler_params=_cparams("arbitrary", "arbitrary"),
        name="swa_attention",
    )(q, k2, k2, k2, v2, v2, v2, k2c, v2c, sink_rows)


def _proj_residual_kernel(a_ref, x_ref, gate_ref, w_ref, o_ref):
    o_ref[...] = x_ref[...] + gate_ref[0] * _dot(a_ref[...], w_ref[...])


def _proj_residual(a, x, seg_fn, tm, gate, w):
    t, d = x.shape
    k = a.shape[1]
    return pl.pallas_call(
        _proj_residual_kernel,
        grid=(t // tm,),
        in_specs=[
            pl.BlockSpec((tm, k), lambda i: (i, 0)),
            pl.BlockSpec((tm, d), lambda i: (i, 0)),
            pl.BlockSpec((1, 1, d), lambda i: (seg_fn(i), 0, 0)),
            pl.BlockSpec((k, d), lambda i: (0, 0)),
        ],
        out_specs=pl.BlockSpec((tm, d), lambda i: (i, 0)),
        out_shape=jax.ShapeDtypeStruct((t, d), F32),
        compiler_params=_cparams("arbitrary"),
        name="proj_residual",
    )(a, x, gate, w)


N_RANKS = PEER_TOPK + 1
RANK_ROWS = 24
_CAND_PAIRS = [(r, c) for r in range(N_RANKS) for c in range(N_RANKS) if (r + 1) * (c + 1) <= N_RANKS]


def _batcher_network(n):
    pairs = []

    def merge(lo, length, r):
        step = 2 * r
        if step < length:
            merge(lo, length, step)
            merge(lo + r, length, step)
            for i in range(lo + r, lo + length - r, step):
                pairs.append((i, i + r))
        else:
            pairs.append((lo, lo + r))

    def sort(lo, length):
        if length > 1:
            half = length // 2
            sort(lo, half)
            sort(lo + half, half)
            merge(lo, length, 1)

    sort(0, n)
    return pairs


_SORT_NETWORK = _batcher_network(N_KEYS // 8)


def _tree_max(vals):
    vals = list(vals)
    while len(vals) > 1:
        nxt = [jnp.maximum(vals[i], vals[i + 1]) for i in range(0, len(vals) - 1, 2)]
        if len(vals) % 2:
            nxt.append(vals[-1])
        vals = nxt
    return vals[0]


def _peer_route_kernel(x_ref, g_ref, sh_ref, sc_ref, wqt_ref, keys_ref,
                       h2_ref, s1_ref, s2_ref, rout_ref, qt_scr, a_scr, b_scr):
    tm = x_ref.shape[0]
    nh = PEER_HEADS
    hb = _modulated(x_ref[...], g_ref[...], sh_ref[0], sc_ref[0]).astype(BF16)
    h2_ref[...] = hb
    qt_scr[...] = _dot_nt(wqt_ref[...], hb).astype(BF16)
    nlc = tm // LANES

    def head_body(h, carry):
        chains = []
        for p, (s_ref, top_scr) in enumerate(((s1_ref, a_scr), (s2_ref, b_scr))):
            hp = h * 2 + p
            qs = qt_scr[pl.ds(pl.multiple_of(hp * N_KEYS, N_KEYS), N_KEYS), :]
            s = _dot(keys_ref[hp], qs)
            s_ref[h] = s
            for lc in range(nlc):
                ls = slice(lc * LANES, (lc + 1) * LANES)
                chains.append((top_scr, ls, s[:, ls]))
        lists = [[x[g * 8:(g + 1) * 8, :] for g in range(N_KEYS // 8)] for (_, _, x) in chains]
        for (i, j) in _SORT_NETWORK:
            for lst in lists:
                hi = jnp.maximum(lst[i], lst[j])
                lst[j] = jnp.minimum(lst[i], lst[j])
                lst[i] = hi
        for r in range(N_RANKS):
            depth = min(len(lists[0]), N_RANKS - r)
            for (top_scr, ls, _), lst in zip(chains, lists):
                m = jnp.max(lst[0], axis=0, keepdims=True)
                top_scr[h, r:r + 1, ls] = m
                popped = lst[0] == m
                for kq in range(depth - 1):
                    lst[kq] = jnp.where(popped, lst[kq + 1], lst[kq])
                lst[depth - 1] = jnp.where(popped, LOWEST, lst[depth - 1])
        return carry

    lax.fori_loop(0, nh, head_body, 0)

    for lc in range(nlc):
        ls = slice(lc * LANES, (lc + 1) * LANES)
        av = [jnp.concatenate([a_scr[h, r:r + 1, ls] for h in range(nh)], axis=0) for r in range(N_RANKS)]
        bv = [jnp.concatenate([b_scr[h, r:r + 1, ls] for h in range(nh)], axis=0) for r in range(N_RANKS)]
        cands = [av[r] + bv[c] for (r, c) in _CAND_PAIRS]
        top = cands[0]
        cur = list(cands)
        kth = []
        for it in range(N_RANKS):
            m = _tree_max(cur)
            kth.append(m)
            cur = [jnp.where(cv == m, LOWEST, cv) for cv in cur]
        tau = 0.5 * (kth[PEER_TOPK - 1] + kth[PEER_TOPK])
        z = jnp.zeros_like(top)
        for cv in cands:
            z = z + jnp.where(cv >= tau, jnp.exp(cv - top), 0.0)
        rout_ref[0 * nh:1 * nh, ls] = tau
        rout_ref[1 * nh:2 * nh, ls] = av[0]
        rout_ref[2 * nh:3 * nh, ls] = bv[0]
        rout_ref[3 * nh:4 * nh, ls] = 1.0 / z


def _peer_route(x, seg_fn, tm, norm_g, shift, scale, wq_t, keys):
    t, d = x.shape
    nq = wq_t.shape[0]
    nh = PEER_HEADS
    sshape = jax.ShapeDtypeStruct((nh, N_KEYS, t), F32)
    sspec = pl.BlockSpec((nh, N_KEYS, tm), lambda i: (0, 0, i))
    return pl.pallas_call(
        _peer_route_kernel,
        grid=(t // tm,),
        in_specs=[
            pl.BlockSpec((tm, d), lambda i: (i, 0)),
            pl.BlockSpec((1, d), lambda i: (0, 0)),
            pl.BlockSpec((1, 1, d), lambda i: (seg_fn(i), 0, 0)),
            pl.BlockSpec((1, 1, d), lambda i: (seg_fn(i), 0, 0)),
            pl.BlockSpec((nq, d), lambda i: (0, 0)),
            pl.BlockSpec(keys.shape, lambda i: (0, 0, 0)),
        ],
        out_specs=[
            pl.BlockSpec((tm, d), lambda i: (i, 0)),
            sspec, sspec,
            pl.BlockSpec((4 * nh, tm), lambda i: (0, i)),
        ],
        out_shape=[jax.ShapeDtypeStruct((t, d), BF16), sshape, sshape,
                   jax.ShapeDtypeStruct((4 * nh, t), F32)],
        scratch_shapes=[
            pltpu.VMEM((nq, tm), BF16),
            pltpu.VMEM((nh, RANK_ROWS, tm), F32),
            pltpu.VMEM((nh, RANK_ROWS, tm), F32),
        ],
        compiler_params=_cparams("arbitrary"),
        name="peer_route",
    )(x, norm_g, shift, scale, wq_t, keys)


PEER_LANE_CHUNK = 128


def _peer_expert_kernel(h2_ref, s1_ref, s2_ref, rout_ref, u_ref, vt_ref, x_ref, gate_ref, fg_ref,
                        o_ref, acc_scr, e2_scr, ht0_scr, ht1_scr, a0_scr, a1_scr, *, final_norm):
    ht_scr = (ht0_scr, ht1_scr)
    a_scr = (a0_scr, a1_scr)
    c = pl.program_id(1)
    nc = pl.num_programs(1)
    nh = PEER_HEADS
    tm = h2_ref.shape[0]
    per_step = EXPERT_CHUNK // N_KEYS

    @pl.when(c == 0)
    def _():
        acc_scr[...] = jnp.zeros_like(acc_scr)
        for h in range(nh):
            e2_scr[h] = (jnp.exp(s2_ref[h] - rout_ref[2 * nh + h:2 * nh + h + 1, :])
                         * rout_ref[3 * nh + h:3 * nh + h + 1, :])

    tau = rout_ref[0:nh, :]
    m1 = rout_ref[nh:2 * nh, :]
    m2 = rout_ref[2 * nh:3 * nh, :]
    inv_z = rout_ref[3 * nh:4 * nh, :]
    lcw = min(PEER_LANE_CHUNK, tm)
    n_pieces = u_ref.shape[0]
    keys_per_piece = PEER_PIECE // N_KEYS

    def hidden(p, slot):
        ht_scr[slot][...] = _dot_nt(u_ref[p], h2_ref[...])

    def gate_piece(p, slot):
        for kk in range(keys_per_piece):
            i = c * per_step + p * keys_per_piece + kk
            s1rows = jnp.concatenate([s1_ref[h, pl.ds(i, 1), :] for h in range(nh)], axis=0)
            thr = jnp.exp(tau - s1rows - m2) * inv_z
            coef = jnp.exp(s1rows - m1)
            erows = slice(kk * N_KEYS, (kk + 1) * N_KEYS)
            for lc in range(tm // lcw):
                ls = slice(lc * lcw, (lc + 1) * lcw)
                gate = jnp.zeros((N_KEYS, lcw), F32)
                for h in range(nh):
                    e2 = e2_scr[h, :, ls]
                    gate = gate + jnp.where(e2 >= thr[h:h + 1, ls], e2, 0.0) * coef[h:h + 1, ls]
                hh = ht_scr[slot][erows, ls]
                act = 0.5 * hh * (1.0 + lax.erf(hh * INV_SQRT2))
                a_scr[slot][erows, ls] = (act * gate).astype(BF16)

    def project(p, slot):
        acc_scr[...] += _dot(vt_ref[p], a_scr[slot][...])

    hidden(0, 0)
    hidden(1, 1)
    gate_piece(0, 0)

    def stage_pair(j, carry):
        p = 2 * j + 2
        project(p - 2, 0)
        hidden(p, 0)
        gate_piece(p - 1, 1)
        project(p - 1, 1)
        hidden(p + 1, 1)
        gate_piece(p, 0)
        return carry

    for pair in range((n_pieces - 2) // 2):
        stage_pair(pair, 0)
    gate_piece(n_pieces - 1, 1)
    project(n_pieces - 2, 0)
    project(n_pieces - 1, 1)

    @pl.when(c == nc - 1)
    def _():
        out = x_ref[...] + gate_ref[0] * acc_scr[...].T
        if final_norm:
            ms = jnp.mean(out * out, axis=-1, keepdims=True)
            out = out * lax.rsqrt(ms + RMS_EPS) * fg_ref[...]
        o_ref[...] = out


def _peer_experts(h2, s1, s2, rout, u_bf, vt_bf, x, seg_fn, tm, gate, final_g, final_norm):
    t, d = x.shape
    ppc = EXPERT_CHUNK // PEER_PIECE
    nh = PEER_HEADS
    sspec = pl.BlockSpec((nh, N_KEYS, tm), lambda i, c: (0, 0, i))
    return pl.pallas_call(
        functools.partial(_peer_expert_kernel, final_norm=final_norm),
        grid=(t // tm, u_bf.shape[0] // ppc),
        in_specs=[
            pl.BlockSpec((tm, d), lambda i, c: (i, 0)),
            sspec, sspec,
            pl.BlockSpec((4 * nh, tm), lambda i, c: (0, i)),
            pl.BlockSpec((ppc, PEER_PIECE, d), lambda i, c: (c, 0, 0)),
            pl.BlockSpec((ppc, d, PEER_PIECE), lambda i, c: (c, 0, 0)),
            pl.BlockSpec((tm, d), lambda i, c: (i, 0)),
            pl.BlockSpec((1, 1, d), lambda i, c: (seg_fn(i), 0, 0)),
            pl.BlockSpec((1, d), lambda i, c: (0, 0)),
        ],
        out_specs=pl.BlockSpec((tm, d), lambda i, c: (i, 0)),
        out_shape=jax.ShapeDtypeStruct((t, d), F32),
        scratch_shapes=[
            pltpu.VMEM((d, tm), F32),
            pltpu.VMEM((nh, N_KEYS, tm), F32),
            pltpu.VMEM((PEER_PIECE, tm), F32),
            pltpu.VMEM((PEER_PIECE, tm), F32),
            pltpu.VMEM((PEER_PIECE, tm), BF16),
            pltpu.VMEM((PEER_PIECE, tm), BF16),
        ],
        compiler_params=_cparams("arbitrary", "arbitrary"),
        name="peer_experts",
    )(h2, s1, s2, rout, u_bf, vt_bf, x, gate, final_g)


def _peer(x, seg_fn, seqlen, norm_g, shift, scale, gate, wq_t, keys, u_bf, vt_bf, final_g, final_norm):
    tm_r = min(256, seqlen)
    tm_e = min(512, seqlen)
    h2, s1, s2, rout = _peer_route(x, lambda i: seg_fn(i, tm_r), tm_r, norm_g, shift, scale, wq_t, keys)
    return _peer_experts(h2, s1, s2, rout, u_bf, vt_bf, x, lambda i: seg_fn(i, tm_e), tm_e, gate,
                         final_g, final_norm)


def _rope_tables(n_lat):
    rows = n_lat // GRID_W
    row = jnp.broadcast_to(jnp.arange(rows)[:, None], (rows, GRID_W)).reshape(-1).astype(F32)
    col = jnp.broadcast_to(jnp.arange(GRID_W)[None, :], (rows, GRID_W)).reshape(-1).astype(F32)
    inv = ROPE_BASE ** (-jnp.arange(0, ROPE_AXIS_DIM, 2, dtype=F32) / ROPE_AXIS_DIM)
    ang_r = row[:, None] * inv
    ang_c = col[:, None] * inv
    cr, sr, cc, sc = jnp.cos(ang_r), jnp.sin(ang_r), jnp.cos(ang_c), jnp.sin(ang_c)
    cos64 = jnp.concatenate([cr, cr, cc, cc], axis=-1)
    sin64 = jnp.concatenate([-sr, sr, -sc, sc], axis=-1)
    reps = LANES // SWA_HEAD_DIM
    return jnp.tile(cos64, (1, reps)), jnp.tile(sin64, (1, reps))


def _swa_weight_layout(w_in):
    qw = SWA_Q_HEADS * SWA_HEAD_DIM
    kvw = SWA_KV_HEADS * SWA_HEAD_DIM
    half = ROPE_AXIS_DIM // 2
    q_cols = np.arange(qw)
    q_swap = (q_cols // SWA_HEAD_DIM) * SWA_HEAD_DIM + ((q_cols % SWA_HEAD_DIM) ^ half)
    dup = np.arange(SWA_KV_HEADS * 2 * SWA_HEAD_DIM)
    kv_dup = (dup // (2 * SWA_HEAD_DIM)) * SWA_HEAD_DIM + (dup % SWA_HEAD_DIM)
    kv_dup_swap = (kv_dup // SWA_HEAD_DIM) * SWA_HEAD_DIM + ((kv_dup % SWA_HEAD_DIM) ^ half)
    wq = w_in[:, :qw]
    wk = w_in[:, qw:qw + kvw]
    wv = w_in[:, qw + kvw:]
    w_cat = jnp.concatenate([wq, wq[:, q_swap], wk[:, kv_dup], wk[:, kv_dup_swap], wv[:, kv_dup]], axis=1)
    w_kv = jnp.concatenate([wk[:, kv_dup], wv[:, kv_dup]], axis=1)
    return w_cat.astype(BF16), w_kv.astype(BF16)


def kernel(x, c, ctx, c_ctx, ada_w, ada_b, norm1_g, norm2_g, gdn_w_in, gdn_conv_w, gdn_a_log, gdn_dt_bias,
           gdn_norm_g, gdn_w_out, swa_w_in, swa_sinks, swa_w_out, peer_w_query, peer_sub_keys, peer_u, peer_v,
           final_g):
    return _forward(x, c, ctx, c_ctx, ada_w, ada_b, norm1_g, norm2_g, gdn_w_in, gdn_conv_w, gdn_a_log,
                    gdn_dt_bias, gdn_norm_g, gdn_w_out, swa_w_in, swa_sinks, swa_w_out, peer_w_query,
                    peer_sub_keys, peer_u, peer_v, final_g)[0]


def _forward(x, c, ctx, c_ctx, ada_w, ada_b, norm1_g, norm2_g, gdn_w_in, gdn_conv_w, gdn_a_log, gdn_dt_bias,
             gdn_norm_g, gdn_w_out, swa_w_in, swa_sinks, swa_w_out, peer_w_query, peer_sub_keys, peer_u, peer_v,
             final_g):
    b, l, d = x.shape
    nctx = ctx.shape[1]
    depth = ada_w.shape[0]
    assert depth == 2 and l % 256 == 0 and nctx % 128 == 0 and b + 1 <= 16

    mod_rows = 16
    cmat = jnp.zeros((mod_rows, d), F32).at[:b].set(c).at[b].set(c_ctx)
    ada = _ada_params(cmat, ada_w, ada_b)
    mods = ada.reshape(depth, mod_rows, ADA_CHUNKS, d).transpose(0, 2, 1, 3)[:, :, :, None, :]

    xl = x.reshape(b * l, d)
    xc = ctx.reshape(b * nctx, d)

    def lat_seg(i, tm):
        return i // (l // tm)

    def ctx_seg(i, tm):
        return b

    streams = ((lat_seg, l), (ctx_seg, nctx))

    def peer_weights(li):
        wq_t = peer_w_query[li].T.astype(BF16)
        keys = peer_sub_keys[li].reshape(PEER_HEADS * 2, N_KEYS, -1).astype(BF16)
        u_p = peer_u[li].astype(BF16).reshape(-1, PEER_PIECE, d)
        vt_p = peer_v[li].astype(BF16).reshape(-1, PEER_PIECE, d).transpose(0, 2, 1)
        return wq_t, keys, u_p, vt_p

    final_row = final_g.reshape(1, d)

    sh1, sc1, g1, sh2, sc2, g2 = (mods[0, j] for j in range(ADA_CHUNKS))
    n1 = norm1_g[0].reshape(1, d)
    n2 = norm2_g[0].reshape(1, d)
    width = GDN_HEADS * GDN_DK
    w_in = gdn_w_in[0]
    w_main = w_in[:, :4 * width].astype(BF16)
    w_abt = w_in[:, 4 * width:].T
    conv_w8 = jnp.zeros((8, 3 * width), F32).at[:GDN_CONV].set(gdn_conv_w[0])
    alog_rows = jnp.broadcast_to(gdn_a_log[0].reshape(2 * GDN_HEADS, 1), (2 * GDN_HEADS, LANES))
    dt_rows = jnp.broadcast_to(gdn_dt_bias[0].reshape(2 * GDN_HEADS, 1), (2 * GDN_HEADS, LANES))
    gdn_ng = gdn_norm_g[0].reshape(1, GDN_DK)
    w_out0 = gdn_w_out[0].astype(BF16)

    tok = {}
    for name, xs, (seg, slen) in (("ctx", xc, streams[1]), ("lat", xl, streams[0])):
        tm = min(512, slen)
        p_main, ab_t = _gdn_inproj(xs, lambda i: seg(i, tm), tm, n1, sh1, sc1, w_main, w_abt)
        qh, kh, vh = _gdn_prep(p_main, b, slen, conv_w8)
        tok[name] = (xs, seg, slen, tm, p_main, ab_t, qh, kh, vh)

    s0 = jnp.zeros((b, 2, GDN_HEADS, GDN_DK, GDN_DK), F32)
    new_x = {}
    for name in ("ctx", "lat"):
        xs, seg, slen, tm, p_main, ab_t, qh, kh, vh = tok[name]
        o_f, o_b, s0 = _gdn_scan(qh, kh, vh, ab_t, b, slen, alog_rows, dt_rows, s0)
        new_x[name] = _gdn_out(o_f, o_b, p_main, xs, lambda i: seg(i, tm), tm, g1, gdn_ng, w_out0)

    wq_t, keys, u_bf, vt_bf = peer_weights(0)
    xl = _peer(new_x["lat"], lat_seg, l, n2, sh2, sc2, g2, wq_t, keys, u_bf, vt_bf, final_row, False)
    xc = _peer(new_x["ctx"], ctx_seg, nctx, n2, sh2, sc2, g2, wq_t, keys, u_bf, vt_bf, final_row, False)
    streams_seen = dict(x1=new_x["lat"], xc1=new_x["ctx"], x2=xl, xc2=xc)

    sh1, sc1, g1, sh2, sc2, g2 = (mods[1, j] for j in range(ADA_CHUNKS))
    n1 = norm1_g[1].reshape(1, d)
    n2 = norm2_g[1].reshape(1, d)
    w_cat, w_kv = _swa_weight_layout(swa_w_in[0])
    cos_t, sin_t = _rope_tables(l)
    tm = min(512, l)
    q, k2, v2 = _swa_inproj(xl, lambda i: lat_seg(i, tm), tm, n1, sh1, sc1, w_cat, cos_t, sin_t)
    tmc = min(512, nctx)
    k2c, v2c = _swa_ctx_kv(xc, lambda i: ctx_seg(i, tmc), tmc, n1, sh1, sc1, w_kv)
    sink_rows = jnp.broadcast_to(swa_sinks[0].reshape(SWA_Q_HEADS, 1), (SWA_Q_HEADS, LANES))
    att = _attention(q, k2, v2, k2c, v2c, sink_rows, b, l, nctx)
    xl = _proj_residual(att, xl, lambda i: lat_seg(i, tm), tm, g1, swa_w_out[0].astype(BF16))
    streams_seen["x3"] = xl
    wq_t, keys, u_bf, vt_bf = peer_weights(1)
    out = _peer(xl, lat_seg, l, n2, sh2, sc2, g2, wq_t, keys, u_bf, vt_bf, final_row, True)
    return out.reshape(b, l, d), streams_seen
```

```python
import functools
import math

import numpy as np
import jax
import jax.numpy as jnp
from jax import lax
from jax.experimental import pallas as pl
from jax.experimental.pallas import tpu as pltpu

F32 = jnp.float32
BF16 = jnp.bfloat16

GRID_W = 64
GDN_HEADS = 8
GDN_DK = 128
GDN_CONV = 5
SWA_Q_HEADS = 16
SWA_KV_HEADS = 4
SWA_HEAD_DIM = 64
SWA_GROUP = SWA_Q_HEADS // SWA_KV_HEADS
ROPE_BASE = 10000.0
ROPE_AXIS_DIM = SWA_HEAD_DIM // 2
PEER_HEADS = 8
N_KEYS = 128
PEER_TOPK = 16
ADA_CHUNKS = 6
RMS_EPS = 1e-6
NEG_INF = -1e30

LANES = 128
SUBLANES_BF16 = 16
VMEM_LIMIT_BYTES = 60 * 1024 * 1024

CHUNK = LANES
ATTN_BLOCK = 128
ATTN_GROUPS_PER_BATCH = 2
EXPERT_CHUNK = 4096
PEER_PIECE = 256
ADA_TN = 1024
LOWEST = -3.0e38
INV_SQRT2 = 0.7071067811865476


def _cparams(*sem):
    return pltpu.CompilerParams(dimension_semantics=sem, vmem_limit_bytes=VMEM_LIMIT_BYTES)


def _dot(a, b):
    return jnp.dot(a, b, preferred_element_type=F32)


def _dot_nt(a, b):
    return lax.dot_general(a, b, (((1,), (1,)), ((), ())), preferred_element_type=F32)


def _dot_tn(a, b):
    return lax.dot_general(a, b, (((0,), (0,)), ((), ())), preferred_element_type=F32)


def _split2(a):
    hi = a.astype(BF16)
    lo = (a - hi.astype(F32)).astype(BF16)
    return hi, lo


def _split3(a):
    a1 = a.astype(BF16)
    r1 = a - a1.astype(F32)
    a2 = r1.astype(BF16)
    a3 = (r1 - a2.astype(F32)).astype(BF16)
    return a1, a2, a3


def _sigmoid(x):
    return 1.0 / (1.0 + jnp.exp(-x))


def _softplus(x):
    return jnp.maximum(x, 0.0) + jnp.log1p(jnp.exp(-jnp.abs(x)))


def _modulated(x, g, shift, scale):
    ms = jnp.mean(x * x, axis=-1, keepdims=True)
    y = x * lax.rsqrt(ms + RMS_EPS)
    return (y * g) * (1.0 + scale) + shift


def _ada_kernel(c_ref, w_ref, b_ref, o_ref):
    c = c_ref[...]
    s = c * _sigmoid(c)
    sh, sl = _split2(s)
    wh, wl = _split2(w_ref[0])
    o_ref[0] = _dot(sh, wh) + _dot(sh, wl) + _dot(sl, wh) + b_ref[0]


def _ada_params(cmat, ada_w, ada_b):
    depth, d, n = ada_w.shape
    rows = cmat.shape[0]
    return pl.pallas_call(
        _ada_kernel,
        grid=(depth, n // ADA_TN),
        in_specs=[
            pl.BlockSpec((rows, d), lambda l, j: (0, 0)),
            pl.BlockSpec((1, d, ADA_TN), lambda l, j: (l, 0, j)),
            pl.BlockSpec((1, 1, ADA_TN), lambda l, j: (l, 0, j)),
        ],
        out_specs=pl.BlockSpec((1, rows, ADA_TN), lambda l, j: (l, 0, j)),
        out_shape=jax.ShapeDtypeStruct((depth, rows, n), F32),
        compiler_params=_cparams("arbitrary", "arbitrary"),
        name="ada_params",
    )(cmat, ada_w, ada_b.reshape(depth, 1, n))


def _gdn_inproj_kernel(x_ref, g_ref, sh_ref, sc_ref, w_ref, wab_ref, p_ref, ab_ref):
    h = _modulated(x_ref[...], g_ref[...], sh_ref[0], sc_ref[0])
    hb = h.astype(BF16)
    p_ref[...] = _dot(hb, w_ref[...]).astype(BF16)
    hl = (h - hb.astype(F32)).astype(BF16)
    wh, wl = _split2(wab_ref[...])
    ab_ref[...] = _dot_nt(wh, hb) + _dot_nt(wh, hl) + _dot_nt(wl, hb)


def _gdn_inproj(x, seg_fn, tm, norm_g, shift, scale, w_main, w_abt):
    t, d = x.shape
    n = w_main.shape[1]
    na = w_abt.shape[0]
    return pl.pallas_call(
        _gdn_inproj_kernel,
        grid=(t // tm,),
        in_specs=[
            pl.BlockSpec((tm, d), lambda i: (i, 0)),
            pl.BlockSpec((1, d), lambda i: (0, 0)),
            pl.BlockSpec((1, 1, d), lambda i: (seg_fn(i), 0, 0)),
            pl.BlockSpec((1, 1, d), lambda i: (seg_fn(i), 0, 0)),
            pl.BlockSpec((d, n), lambda i: (0, 0)),
            pl.BlockSpec((na, d), lambda i: (0, 0)),
        ],
        out_specs=[
            pl.BlockSpec((tm, n), lambda i: (i, 0)),
            pl.BlockSpec((na, tm), lambda i: (0, i)),
        ],
        out_shape=[jax.ShapeDtypeStruct((t, n), BF16), jax.ShapeDtypeStruct((na, t), F32)],
        compiler_params=_cparams("arbitrary"),
        name="gdn_inproj",
    )(x, norm_g, shift, scale, w_main, w_abt)


CONV_HALO = SUBLANES_BF16
CONV_PAD = GDN_CONV // 2


def _gdn_prep_kernel(main_ref, prev_ref, next_ref, w_ref, q_ref, k_ref, v_ref, ext_scr):
    t = pl.program_id(1)
    nt = pl.num_programs(1)
    tl = main_ref.shape[0]
    base = CONV_HALO
    ext_scr[base:base + tl, :] = main_ref[...].astype(F32)
    prev = prev_ref[...].astype(F32)[CONV_HALO - CONV_PAD:, :]
    nxt = next_ref[...].astype(F32)[:CONV_PAD, :]
    ext_scr[base - CONV_PAD:base, :] = prev * (t > 0).astype(F32)
    ext_scr[base + tl:base + tl + CONV_PAD, :] = nxt * (t < nt - 1).astype(F32)
    nblk = main_ref.shape[1] // LANES
    for cb in range(nblk):
        cols = slice(cb * LANES, (cb + 1) * LANES)
        acc = jnp.zeros((tl, LANES), F32)
        for j in range(GDN_CONV):
            acc = acc + w_ref[j:j + 1, cols] * ext_scr[base - CONV_PAD + j:base - CONV_PAD + j + tl, cols]
        y = acc * _sigmoid(acc)
        which, head = divmod(cb, GDN_HEADS)
        if which < 2:
            y = y * lax.rsqrt(jnp.sum(y * y, axis=-1, keepdims=True) + 1e-6)
            if which == 0:
                y = y * (GDN_DK ** -0.5)
        (q_ref, k_ref, v_ref)[which][head] = y.astype(BF16)


def _gdn_prep(p_main, nseq, seqlen, conv_w8):
    t = p_main.shape[0]
    tl = min(256, seqlen)
    tps = seqlen // tl
    ncol = 3 * GDN_HEADS * GDN_DK
    hb = tl // CONV_HALO
    last_halo = t // CONV_HALO - 1
    out = jax.ShapeDtypeStruct((GDN_HEADS, t, GDN_DK), BF16)
    ospec = pl.BlockSpec((GDN_HEADS, tl, GDN_DK), lambda s, i: (0, s * tps + i, 0))
    return pl.pallas_call(
        _gdn_prep_kernel,
        grid=(nseq, tps),
        in_specs=[
            pl.BlockSpec((tl, ncol), lambda s, i: (s * tps + i, 0)),
            pl.BlockSpec((CONV_HALO, ncol), lambda s, i: (jnp.maximum((s * tps + i) * hb - 1, 0), 0)),
            pl.BlockSpec((CONV_HALO, ncol), lambda s, i: (jnp.minimum((s * tps + i + 1) * hb, last_halo), 0)),
            pl.BlockSpec((8, ncol), lambda s, i: (0, 0)),
        ],
        out_specs=[ospec, ospec, ospec],
        out_shape=[out, out, out],
        scratch_shapes=[pltpu.VMEM((tl + 2 * CONV_HALO, ncol), F32)],
        compiler_params=_cparams("arbitrary", "arbitrary"),
        name="gdn_prep",
    )(p_main, p_main, p_main, conv_w8)


INV_BASE = 16


def _each(fn, *lists):
    return [fn(*args) for args in zip(*lists)]


def _unit_triangular_inverses(lmats, ri, ci):
    c = lmats[0].shape[0]
    eye = jnp.where(ri == ci, 1.0, 0.0)
    same_base = (ri // INV_BASE) == (ci // INV_BASE)
    a = [jnp.where(same_base, -l, 0.0) for l in lmats]
    t = [eye + x for x in a]
    ap = [x.astype(BF16) for x in a]
    for _ in range(1, int(math.log2(INV_BASE))):
        ap = [_dot(x, x).astype(BF16) for x in ap]
        t = _each(lambda tt, x: tt + _dot(tt.astype(BF16), x), t, ap)
    size = INV_BASE
    while size < c:
        same_small = (ri // size) == (ci // size)
        same_big = (ri // (2 * size)) == (ci // (2 * size))
        couple = jnp.logical_and(same_big, jnp.logical_not(same_small))
        l1 = [jnp.where(couple, l, 0.0).astype(BF16) for l in lmats]
        tb = [x.astype(BF16) for x in t]
        mid = _each(lambda l, x: _dot(l, x).astype(BF16), l1, tb)
        t = _each(lambda tt, x, m: tt - _dot(x, m), t, tb, mid)
        size *= 2
    return t


def _delta_chunks(q, k, v, gc_row, beta_row, tot_row, s_prev, lower):
    c = q[0].shape[0]
    dk = k[0].shape[1]
    ri = lax.broadcasted_iota(jnp.int32, (c, c), 0)
    ci = lax.broadcasted_iota(jnp.int32, (c, c), 1)
    incl = [ri >= ci if lo else ri <= ci for lo in lower]
    strict = [ri > ci if lo else ri < ci for lo in lower]
    gc_rb = [jnp.broadcast_to(g, (c, c)) for g in gc_row]
    gc_cb = [g.T for g in gc_rb]
    beta_cb = [jnp.broadcast_to(b, (c, c)).T for b in beta_row]
    tot_b = [jnp.broadcast_to(t, (c, c)) for t in tot_row]
    decay = _each(lambda m, gc, gr: jnp.where(m, jnp.exp(jnp.where(m, gc - gr, 0.0)), 0.0), incl, gc_cb, gc_rb)
    kf = [x.astype(F32) for x in k]
    kk = _each(_dot_nt, k, k)
    qk = _each(_dot_nt, q, k)
    lmat = _each(lambda x, b, m, d: (x * b) * jnp.where(m, d, 0.0), kk, beta_cb, strict, decay)
    aqk = _each(lambda x, d: (x * d).astype(BF16), qk, decay)
    eg_cb = [jnp.exp(g) for g in gc_cb]
    rhs = _each(lambda vv, kx, b, e: jnp.concatenate([vv.astype(F32) * b, kx * (b * e)], axis=1).astype(BF16),
                v, kf, beta_cb, eg_cb)
    t_inv = _unit_triangular_inverses(lmat, ri, ci)
    x = _each(lambda t, r: _dot(t.astype(BF16), r), t_inv, rhs)
    sb = [s.astype(BF16) for s in s_prev]
    v_new = _each(lambda xx, s: xx[:, :dk] - _dot(xx[:, dk:].astype(BF16), s), x, sb)
    vb = [vn.astype(BF16) for vn in v_new]
    qd = _each(lambda qq, e: (qq.astype(F32) * e).astype(BF16), q, eg_cb)
    o = _each(lambda a, s, w, vv: _dot(a, s) + _dot(w, vv), qd, sb, aqk, vb)
    kd = _each(lambda kx, t, g: (kx * jnp.exp(t - g)).astype(BF16), kf, tot_b, gc_cb)
    s_next = _each(lambda s, t, a, vv: s * jnp.exp(t) + _dot_tn(a, vv), s_prev, tot_b, kd, vb)
    return o, s_next


def _gdn_scan_kernel(qf_ref, kf_ref, vf_ref, qb_ref, kb_ref, vb_ref, abf_ref, abb_ref, alog_ref, dt_ref,
                     s0_ref, of_ref, ob_ref, sfin_ref, s_scr):
    i = pl.program_id(1)
    n = pl.num_programs(1)
    nh = GDN_HEADS

    @pl.when(i == 0)
    def _():
        s_scr[...] = s0_ref[0]

    c = CHUNK
    ri = lax.broadcasted_iota(jnp.int32, (c, c), 0)
    ci = lax.broadcasted_iota(jnp.int32, (c, c), 1)
    one = jnp.ones((c, c), BF16)
    upper_incl = jnp.where(ri <= ci, 1.0, 0.0).astype(BF16)
    lower_incl = jnp.where(ri >= ci, 1.0, 0.0).astype(BF16)

    def cums(g, tri):
        g1, g2, g3 = _split3(g)
        return (_dot(g1, tri) + _dot(g2, tri) + _dot(g3, tri),
                _dot(g1, one) + _dot(g2, one) + _dot(g3, one))

    g_f = -jnp.exp(alog_ref[0:nh, :]) * _softplus(abf_ref[0:nh, :] + dt_ref[0:nh, :])
    g_b = -jnp.exp(alog_ref[nh:2 * nh, :]) * _softplus(abb_ref[nh:2 * nh, :] + dt_ref[nh:2 * nh, :])
    gc_f, tot_f = cums(g_f, upper_incl)
    gc_b, tot_b = cums(g_b, lower_incl)
    rows = jnp.concatenate([gc_f, gc_b, _sigmoid(abf_ref[2 * nh:3 * nh, :]), _sigmoid(abb_ref[3 * nh:4 * nh, :]),
                            tot_f, tot_b], axis=0)

    row = lambda r: rows[r:r + 1]
    heads = range(nh)
    o, s_next = _delta_chunks(
        q=[qf_ref[h] for h in heads] + [qb_ref[h] for h in heads],
        k=[kf_ref[h] for h in heads] + [kb_ref[h] for h in heads],
        v=[vf_ref[h] for h in heads] + [vb_ref[h] for h in heads],
        gc_row=[row(h) for h in heads] + [row(nh + h) for h in heads],
        beta_row=[row(2 * nh + h) for h in heads] + [row(3 * nh + h) for h in heads],
        tot_row=[row(4 * nh + h) for h in heads] + [row(5 * nh + h) for h in heads],
        s_prev=[s_scr[0, h] for h in heads] + [s_scr[1, h] for h in heads],
        lower=[True] * nh + [False] * nh)
    for h in heads:
        s_scr[0, h] = s_next[h]
        s_scr[1, h] = s_next[nh + h]
        of_ref[h] = o[h].astype(BF16)
        ob_ref[h] = o[nh + h].astype(BF16)

    @pl.when(i == n - 1)
    def _():
        sfin_ref[0] = s_scr[...]


def _gdn_scan(qh, kh, vh, ab_t, nseq, seqlen, alog_rows, dt_rows, s0):
    t = qh.shape[1]
    nt = seqlen // CHUNK
    nab = ab_t.shape[0]
    hspec_f = pl.BlockSpec((GDN_HEADS, CHUNK, GDN_DK), lambda s, i: (0, s * nt + i, 0))
    hspec_b = pl.BlockSpec((GDN_HEADS, CHUNK, GDN_DK), lambda s, i: (0, s * nt + nt - 1 - i, 0))
    sspec = pl.BlockSpec((1, 2, GDN_HEADS, GDN_DK, GDN_DK), lambda s, i: (s, 0, 0, 0, 0))
    oshape = jax.ShapeDtypeStruct((GDN_HEADS, t, GDN_DK), BF16)
    return pl.pallas_call(
        _gdn_scan_kernel,
        grid=(nseq, nt),
        in_specs=[
            hspec_f, hspec_f, hspec_f, hspec_b, hspec_b, hspec_b,
            pl.BlockSpec((nab, CHUNK), lambda s, i: (0, s * nt + i)),
            pl.BlockSpec((nab, CHUNK), lambda s, i: (0, s * nt + nt - 1 - i)),
            pl.BlockSpec((2 * GDN_HEADS, LANES), lambda s, i: (0, 0)),
            pl.BlockSpec((2 * GDN_HEADS, LANES), lambda s, i: (0, 0)),
            sspec,
        ],
        out_specs=[hspec_f, hspec_b, sspec],
        out_shape=[oshape, oshape, jax.ShapeDtypeStruct(s0.shape, F32)],
        scratch_shapes=[
            pltpu.VMEM((2, GDN_HEADS, GDN_DK, GDN_DK), F32),
        ],
        compiler_params=_cparams("arbitrary", "arbitrary"),
        name="gdn_scan",
    )(qh, kh, vh, qh, kh, vh, ab_t, ab_t, alog_rows, dt_rows, s0)


def _gdn_out_kernel(of_ref, ob_ref, z_ref, x_ref, gate_ref, ng_ref, w_ref, o_ref):
    parts = []
    for h in range(GDN_HEADS):
        o = of_ref[h].astype(F32) + ob_ref[h].astype(F32)
        on = o * lax.rsqrt(jnp.mean(o * o, axis=-1, keepdims=True) + RMS_EPS) * ng_ref[...]
        z = z_ref[:, h * GDN_DK:(h + 1) * GDN_DK].astype(F32)
        parts.append((on * (z * _sigmoid(z))).astype(BF16))
    a = jnp.concatenate(parts, axis=-1)
    o_ref[...] = x_ref[...] + gate_ref[0] * _dot(a, w_ref[...])


def _gdn_out(o_f, o_b, p_main, x, seg_fn, tm, gate, norm_g, w_out):
    t, d = x.shape
    width = GDN_HEADS * GDN_DK
    zblk = 3
    hspec = pl.BlockSpec((GDN_HEADS, tm, GDN_DK), lambda i: (0, i, 0))
    return pl.pallas_call(
        _gdn_out_kernel,
        grid=(t // tm,),
        in_specs=[
            hspec, hspec,
            pl.BlockSpec((tm, width), lambda i: (i, zblk)),
            pl.BlockSpec((tm, d), lambda i: (i, 0)),
            pl.BlockSpec((1, 1, d), lambda i: (seg_fn(i), 0, 0)),
            pl.BlockSpec((1, GDN_DK), lambda i: (0, 0)),
            pl.BlockSpec((width, d), lambda i: (0, 0)),
        ],
        out_specs=pl.BlockSpec((tm, d), lambda i: (i, 0)),
        out_shape=jax.ShapeDtypeStruct((t, d), F32),
        compiler_params=_cparams("arbitrary"),
        name="gdn_out",
    )(o_f, o_b, p_main, x, gate, norm_g, w_out)


def _swa_inproj_kernel(x_ref, g_ref, sh_ref, sc_ref, w_ref, cos_ref, sin_ref, q_ref, k_ref, v_ref):
    h = _modulated(x_ref[...], g_ref[...], sh_ref[0], sc_ref[0]).astype(BF16)
    p = _dot(h, w_ref[...])
    qw = SWA_Q_HEADS * SWA_HEAD_DIM
    kw = SWA_KV_HEADS * 2 * SWA_HEAD_DIM
    cos = cos_ref[...]
    sin = sin_ref[...]
    cq = jnp.concatenate([cos] * (qw // LANES), axis=-1)
    sq = jnp.concatenate([sin] * (qw // LANES), axis=-1)
    ck = jnp.concatenate([cos] * (kw // LANES), axis=-1)
    sk = jnp.concatenate([sin] * (kw // LANES), axis=-1)
    o = 0
    q = p[:, o:o + qw] * cq + p[:, o + qw:o + 2 * qw] * sq
    o += 2 * qw
    k = p[:, o:o + kw] * ck + p[:, o + kw:o + 2 * kw] * sk
    o += 2 * kw
    q_ref[...] = (q * (SWA_HEAD_DIM ** -0.5)).astype(BF16)
    k_ref[...] = k.astype(BF16)
    v_ref[...] = p[:, o:o + kw].astype(BF16)


def _swa_inproj(x, seg_fn, tm, norm_g, shift, scale, w_cat, cos_t, sin_t):
    t, d = x.shape
    n = w_cat.shape[1]
    qw = SWA_Q_HEADS * SWA_HEAD_DIM
    kw = SWA_KV_HEADS * 2 * SWA_HEAD_DIM
    tps = cos_t.shape[0] // tm
    return pl.pallas_call(
        _swa_inproj_kernel,
        grid=(t // tm,),
        in_specs=[
            pl.BlockSpec((tm, d), lambda i: (i, 0)),
            pl.BlockSpec((1, d), lambda i: (0, 0)),
            pl.BlockSpec((1, 1, d), lambda i: (seg_fn(i), 0, 0)),
            pl.BlockSpec((1, 1, d), lambda i: (seg_fn(i), 0, 0)),
            pl.BlockSpec((d, n), lambda i: (0, 0)),
            pl.BlockSpec((tm, LANES), lambda i: (i % tps, 0)),
            pl.BlockSpec((tm, LANES), lambda i: (i % tps, 0)),
        ],
        out_specs=[
            pl.BlockSpec((tm, qw), lambda i: (i, 0)),
            pl.BlockSpec((tm, kw), lambda i: (i, 0)),
            pl.BlockSpec((tm, kw), lambda i: (i, 0)),
        ],
        out_shape=[jax.ShapeDtypeStruct((t, qw), BF16), jax.ShapeDtypeStruct((t, kw), BF16),
                   jax.ShapeDtypeStruct((t, kw), BF16)],
        compiler_params=_cparams("arbitrary"),
        name="swa_inproj",
    )(x, norm_g, shift, scale, w_cat, cos_t, sin_t)


def _swa_ctx_kv_kernel(x_ref, g_ref, sh_ref, sc_ref, w_ref, k_ref, v_ref):
    h = _modulated(x_ref[...], g_ref[...], sh_ref[0], sc_ref[0]).astype(BF16)
    p = _dot(h, w_ref[...])
    kw = k_ref.shape[1]
    k_ref[...] = p[:, :kw].astype(BF16)
    v_ref[...] = p[:, kw:].astype(BF16)


def _swa_ctx_kv(xc, seg_fn, tm, norm_g, shift, scale, w_kv):
    t, d = xc.shape
    kw = SWA_KV_HEADS * 2 * SWA_HEAD_DIM
    return pl.pallas_call(
        _swa_ctx_kv_kernel,
        grid=(t // tm,),
        in_specs=[
            pl.BlockSpec((tm, d), lambda i: (i, 0)),
            pl.BlockSpec((1, d), lambda i: (0, 0)),
            pl.BlockSpec((1, 1, d), lambda i: (seg_fn(i), 0, 0)),
            pl.BlockSpec((1, 1, d), lambda i: (seg_fn(i), 0, 0)),
            pl.BlockSpec((d, 2 * kw), lambda i: (0, 0)),
        ],
        out_specs=[pl.BlockSpec((tm, kw), lambda i: (i, 0)), pl.BlockSpec((tm, kw), lambda i: (i, 0))],
        out_shape=[jax.ShapeDtypeStruct((t, kw), BF16), jax.ShapeDtypeStruct((t, kw), BF16)],
        compiler_params=_cparams("arbitrary"),
        name="swa_ctx_kv",
    )(xc, norm_g, shift, scale, w_kv)


def _attn_kernel(q_ref, kp_ref, kc_ref, kn_ref, vp_ref, vc_ref, vn_ref, kx_ref, vx_ref, sink_ref, o_ref):
    n = pl.program_id(1)
    nb = pl.num_programs(1)
    blk = ATTN_BLOCK
    nctx = kx_ref.shape[0]
    ri = lax.broadcasted_iota(jnp.int32, (blk, blk), 0)
    ci = lax.broadcasted_iota(jnp.int32, (blk, blk), 1)
    never = 2 * blk
    ok_prev = (ci - ri) >= jnp.where(n > 0, 0, never)
    ok_next = (ri - ci) >= jnp.where(n < nb - 1, 0, never)
    lane = lax.broadcasted_iota(jnp.int32, (1, LANES), 1)
    lo_half = lane < SWA_HEAD_DIM
    keep_lo = jnp.where(lo_half, 1.0, 0.0).astype(BF16)
    keep_hi = jnp.where(lo_half, 0.0, 1.0).astype(BF16)
    gw = 2 * SWA_HEAD_DIM
    for g0 in range(0, SWA_KV_HEADS, ATTN_GROUPS_PER_BATCH):
        qps, kms, vcats, sinks, cols = [], [], [], [], []
        for g in range(g0, g0 + ATTN_GROUPS_PER_BATCH):
            gs = slice(g * gw, (g + 1) * gw)
            kcat = jnp.concatenate([kp_ref[:, gs], kc_ref[:, gs], kn_ref[:, gs], kx_ref[:, gs]], axis=0)
            vcat = jnp.concatenate([vp_ref[:, gs], vc_ref[:, gs], vn_ref[:, gs], vx_ref[:, gs]], axis=0)
            k_halves = (kcat * keep_lo, kcat * keep_hi)
            for pr in range(SWA_GROUP // 2):
                col = (g * (SWA_GROUP // 2) + pr) * LANES
                qp = q_ref[:, col:col + LANES]
                for half in range(2):
                    head = g * SWA_GROUP + pr * 2 + half
                    qps.append(qp)
                    kms.append(k_halves[half])
                    vcats.append(vcat)
                    sinks.append(jnp.broadcast_to(sink_ref[head:head + 1, :], (blk, LANES))[:, 0:1])
                    cols.append(col)
        s = _each(_dot_nt, qps, kms)
        s = [jnp.concatenate([jnp.where(ok_prev, x[:, :blk], NEG_INF), x[:, blk:2 * blk],
                              jnp.where(ok_next, x[:, 2 * blk:3 * blk], NEG_INF), x[:, 3 * blk:]], axis=1) for x in s]
        m = _each(lambda x, sk: jnp.maximum(jnp.max(x, axis=-1, keepdims=True), sk), s, sinks)
        p = _each(lambda x, mm: jnp.exp(x - mm), s, m)
        den = _each(lambda x, sk, mm: jnp.sum(x, axis=-1, keepdims=True) + jnp.exp(sk - mm), p, sinks, m)
        o = _each(lambda x, vv, dd: _dot(x.astype(BF16), vv) / dd, p, vcats, den)
        for j in range(0, len(o), 2):
            o_ref[:, cols[j]:cols[j] + LANES] = jnp.where(lo_half, o[j], o[j + 1]).astype(BF16)


def _attention(q, k2, v2, k2c, v2c, sink_rows, nseq, seqlen, nctx):
    t, qw = q.shape
    kw = k2.shape[1]
    blk = ATTN_BLOCK
    nb = seqlen // blk
    kv_prev = pl.BlockSpec((blk, kw), lambda s, i: (s * nb + jnp.maximum(i - 1, 0), 0))
    kv_cur = pl.BlockSpec((blk, kw), lambda s, i: (s * nb + i, 0))
    kv_next = pl.BlockSpec((blk, kw), lambda s, i: (s * nb + jnp.minimum(i + 1, nb - 1), 0))
    kv_ctx = pl.BlockSpec((nctx, kw), lambda s, i: (s, 0))
    return pl.pallas_call(
        _attn_kernel,
        grid=(nseq, nb),
        in_specs=[
            pl.BlockSpec((blk, qw), lambda s, i: (s * nb + i, 0)),
            kv_prev, kv_cur, kv_next, kv_prev, kv_cur, kv_next, kv_ctx, kv_ctx,
            pl.BlockSpec(sink_rows.shape, lambda s, i: (0, 0)),
        ],
        out_specs=pl.BlockSpec((blk, qw), lambda s, i: (s * nb + i, 0)),
        out_shape=jax.ShapeDtypeStruct((t, qw), BF16),
        compiler_params=_cparams("arbitrary", "arbitrary"),
        name="swa_attention",
    )(q, k2, k2, k2, v2, v2, v2, k2c, v2c, sink_rows)


def _proj_residual_kernel(a_ref, x_ref, gate_ref, w_ref, o_ref):
    o_ref[...] = x_ref[...] + gate_ref[0] * _dot(a_ref[...], w_ref[...])


def _proj_residual(a, x, seg_fn, tm, gate, w):
    t, d = x.shape
    k = a.shape[1]
    return pl.pallas_call(
        _proj_residual_kernel,
        grid=(t // tm,),
        in_specs=[
            pl.BlockSpec((tm, k), lambda i: (i, 0)),
            pl.BlockSpec((tm, d), lambda i: (i, 0)),
            pl.BlockSpec((1, 1, d), lambda i: (seg_fn(i), 0, 0)),
            pl.BlockSpec((k, d), lambda i: (0, 0)),
        ],
        out_specs=pl.BlockSpec((tm, d), lambda i: (i, 0)),
        out_shape=jax.ShapeDtypeStruct((t, d), F32),
        compiler_params=_cparams("arbitrary"),
        name="proj_residual",
    )(a, x, gate, w)


N_RANKS = PEER_TOPK + 1
RANK_ROWS = 24
_CAND_PAIRS = [(r, c) for r in range(N_RANKS) for c in range(N_RANKS) if (r + 1) * (c + 1) <= N_RANKS]


def _batcher_network(n):
    pairs = []

    def merge(lo, length, r):
        step = 2 * r
        if step < length:
            merge(lo, length, step)
            merge(lo + r, length, step)
            for i in range(lo + r, lo + length - r, step):
                pairs.append((i, i + r))
        else:
            pairs.append((lo, lo + r))

    def sort(lo, length):
        if length > 1:
            half = length // 2
            sort(lo, half)
            sort(lo + half, half)
            merge(lo, length, 1)

    sort(0, n)
    return pairs


_SORT_NETWORK = _batcher_network(N_KEYS // 8)


def _tree_max(vals):
    vals = list(vals)
    while len(vals) > 1:
        nxt = [jnp.maximum(vals[i], vals[i + 1]) for i in range(0, len(vals) - 1, 2)]
        if len(vals) % 2:
            nxt.append(vals[-1])
        vals = nxt
    return vals[0]


def _peer_route_kernel(x_ref, g_ref, sh_ref, sc_ref, wqt_ref, keys_ref,
                       h2_ref, s1_ref, s2_ref, rout_ref, a_scr, b_scr):
    tm = x_ref.shape[0]
    nh = PEER_HEADS
    hb = _modulated(x_ref[...], g_ref[...], sh_ref[0], sc_ref[0]).astype(BF16)
    h2_ref[...] = hb
    nlc = tm // LANES
    rows_per_head = 2 * N_KEYS

    def head_body(h):
        qt = _dot_nt(wqt_ref[h * rows_per_head:(h + 1) * rows_per_head, :], hb).astype(BF16)
        chains = []
        for p, (s_ref, top_scr) in enumerate(((s1_ref, a_scr), (s2_ref, b_scr))):
            hp = h * 2 + p
            s = _dot(keys_ref[hp], qt[p * N_KEYS:(p + 1) * N_KEYS, :])
            s_ref[h] = s
            for lc in range(nlc):
                ls = slice(lc * LANES, (lc + 1) * LANES)
                chains.append((top_scr, ls, s[:, ls]))
        lists = [[x[g * 8:(g + 1) * 8, :] for g in range(N_KEYS // 8)] for (_, _, x) in chains]
        for (i, j) in _SORT_NETWORK:
            for lst in lists:
                hi = jnp.maximum(lst[i], lst[j])
                lst[j] = jnp.minimum(lst[i], lst[j])
                lst[i] = hi
        for r in range(N_RANKS):
            depth = min(len(lists[0]), N_RANKS - r)
            for (top_scr, ls, _), lst in zip(chains, lists):
                m = jnp.max(lst[0], axis=0, keepdims=True)
                top_scr[h, r:r + 1, ls] = m
                popped = lst[0] == m
                for kq in range(depth - 1):
                    lst[kq] = jnp.where(popped, lst[kq + 1], lst[kq])
                lst[depth - 1] = jnp.where(popped, LOWEST, lst[depth - 1])

    for h in range(nh):
        head_body(h)

    for lc in range(nlc):
        ls = slice(lc * LANES, (lc + 1) * LANES)
        av = [jnp.concatenate([a_scr[h, r:r + 1, ls] for h in range(nh)], axis=0) for r in range(N_RANKS)]
        bv = [jnp.concatenate([b_scr[h, r:r + 1, ls] for h in range(nh)], axis=0) for r in range(N_RANKS)]
        cands = [av[r] + bv[c] for (r, c) in _CAND_PAIRS]
        top = cands[0]
        cur = list(cands)
        kth = []
        for it in range(N_RANKS):
            m = _tree_max(cur)
            kth.append(m)
            cur = [jnp.where(cv == m, LOWEST, cv) for cv in cur]
        tau = 0.5 * (kth[PEER_TOPK - 1] + kth[PEER_TOPK])
        z = jnp.zeros_like(top)
        for cv in cands:
            z = z + jnp.where(cv >= tau, jnp.exp(cv - top), 0.0)
        rout_ref[0 * nh:1 * nh, ls] = tau
        rout_ref[1 * nh:2 * nh, ls] = av[0]
        rout_ref[2 * nh:3 * nh, ls] = bv[0]
        rout_ref[3 * nh:4 * nh, ls] = 1.0 / z


def _peer_route(x, seg_fn, tm, norm_g, shift, scale, wq_t, keys):
    t, d = x.shape
    nq = wq_t.shape[0]
    nh = PEER_HEADS
    sshape = jax.ShapeDtypeStruct((nh, N_KEYS, t), F32)
    sspec = pl.BlockSpec((nh, N_KEYS, tm), lambda i: (0, 0, i))
    return pl.pallas_call(
        _peer_route_kernel,
        grid=(t // tm,),
        in_specs=[
            pl.BlockSpec((tm, d), lambda i: (i, 0)),
            pl.BlockSpec((1, d), lambda i: (0, 0)),
            pl.BlockSpec((1, 1, d), lambda i: (seg_fn(i), 0, 0)),
            pl.BlockSpec((1, 1, d), lambda i: (seg_fn(i), 0, 0)),
            pl.BlockSpec((nq, d), lambda i: (0, 0)),
            pl.BlockSpec(keys.shape, lambda i: (0, 0, 0)),
        ],
        out_specs=[
            pl.BlockSpec((tm, d), lambda i: (i, 0)),
            sspec, sspec,
            pl.BlockSpec((4 * nh, tm), lambda i: (0, i)),
        ],
        out_shape=[jax.ShapeDtypeStruct((t, d), BF16), sshape, sshape,
                   jax.ShapeDtypeStruct((4 * nh, t), F32)],
        scratch_shapes=[
            pltpu.VMEM((nh, RANK_ROWS, tm), F32),
            pltpu.VMEM((nh, RANK_ROWS, tm), F32),
        ],
        compiler_params=_cparams("arbitrary"),
        name="peer_route",
    )(x, norm_g, shift, scale, wq_t, keys)


PEER_LANE_CHUNK = 128


def _peer_expert_kernel(h2_ref, s1_ref, s2_ref, rout_ref, u_ref, vt_ref, x_ref, gate_ref, fg_ref,
                        o_ref, acc_scr, e2_scr, ht0_scr, ht1_scr, a0_scr, a1_scr, *, final_norm):
    ht_scr = (ht0_scr, ht1_scr)
    a_scr = (a0_scr, a1_scr)
    c = pl.program_id(1)
    nc = pl.num_programs(1)
    nh = PEER_HEADS
    tm = h2_ref.shape[0]
    per_step = EXPERT_CHUNK // N_KEYS

    @pl.when(c == 0)
    def _():
        acc_scr[...] = jnp.zeros_like(acc_scr)
        for h in range(nh):
            e2_scr[h] = (jnp.exp(s2_ref[h] - rout_ref[2 * nh + h:2 * nh + h + 1, :])
                         * rout_ref[3 * nh + h:3 * nh + h + 1, :])

    tau = rout_ref[0:nh, :]
    m1 = rout_ref[nh:2 * nh, :]
    m2 = rout_ref[2 * nh:3 * nh, :]
    inv_z = rout_ref[3 * nh:4 * nh, :]
    lcw = min(PEER_LANE_CHUNK, tm)
    n_pieces = u_ref.shape[0]
    keys_per_piece = PEER_PIECE // N_KEYS

    def hidden(p, slot):
        ht_scr[slot][...] = _dot_nt(u_ref[p], h2_ref[...])

    def gate_piece(p, slot):
        for kk in range(keys_per_piece):
            i = c * per_step + p * keys_per_piece + kk
            s1rows = jnp.concatenate([s1_ref[h, pl.ds(i, 1), :] for h in range(nh)], axis=0)
            thr = jnp.exp(tau - s1rows - m2) * inv_z
            coef = jnp.exp(s1rows - m1)
            erows = slice(kk * N_KEYS, (kk + 1) * N_KEYS)
            for lc in range(tm // lcw):
                ls = slice(lc * lcw, (lc + 1) * lcw)
                gate = jnp.zeros((N_KEYS, lcw), F32)
                for h in range(nh):
                    e2 = e2_scr[h, :, ls]
                    gate = gate + jnp.where(e2 >= thr[h:h + 1, ls], e2, 0.0) * coef[h:h + 1, ls]
                hh = ht_scr[slot][erows, ls]
                act = 0.5 * hh * (1.0 + lax.erf(hh * INV_SQRT2))
                a_scr[slot][erows, ls] = (act * gate).astype(BF16)

    def project(p, slot):
        acc_scr[...] += _dot(vt_ref[p], a_scr[slot][...])

    hidden(0, 0)
    hidden(1, 1)
    gate_piece(0, 0)

    def stage_pair(j, carry):
        p = 2 * j + 2
        project(p - 2, 0)
        hidden(p, 0)
        gate_piece(p - 1, 1)
        project(p - 1, 1)
        hidden(p + 1, 1)
        gate_piece(p, 0)
        return carry

    for pair in range((n_pieces - 2) // 2):
        stage_pair(pair, 0)
    gate_piece(n_pieces - 1, 1)
    project(n_pieces - 2, 0)
    project(n_pieces - 1, 1)

    @pl.when(c == nc - 1)
    def _():
        out = x_ref[...] + gate_ref[0] * acc_scr[...].T
        if final_norm:
            ms = jnp.mean(out * out, axis=-1, keepdims=True)
            out = out * lax.rsqrt(ms + RMS_EPS) * fg_ref[...]
        o_ref[...] = out


def _peer_experts(h2, s1, s2, rout, u_bf, vt_bf, x, seg_fn, tm, gate, final_g, final_norm):
    t, d = x.shape
    ppc = EXPERT_CHUNK // PEER_PIECE
    nh = PEER_HEADS
    sspec = pl.BlockSpec((nh, N_KEYS, tm), lambda i, c: (0, 0, i))
    return pl.pallas_call(
        functools.partial(_peer_expert_kernel, final_norm=final_norm),
        grid=(t // tm, u_bf.shape[0] // ppc),
        in_specs=[
            pl.BlockSpec((tm, d), lambda i, c: (i, 0)),
            sspec, sspec,
            pl.BlockSpec((4 * nh, tm), lambda i, c: (0, i)),
            pl.BlockSpec((ppc, PEER_PIECE, d), lambda i, c: (c, 0, 0)),
            pl.BlockSpec((ppc, d, PEER_PIECE), lambda i, c: (c, 0, 0)),
            pl.BlockSpec((tm, d), lambda i, c: (i, 0)),
            pl.BlockSpec((1, 1, d), lambda i, c: (seg_fn(i), 0, 0)),
            pl.BlockSpec((1, d), lambda i, c: (0, 0)),
        ],
        out_specs=pl.BlockSpec((tm, d), lambda i, c: (i, 0)),
        out_shape=jax.ShapeDtypeStruct((t, d), F32),
        scratch_shapes=[
            pltpu.VMEM((d, tm), F32),
            pltpu.VMEM((nh, N_KEYS, tm), F32),
            pltpu.VMEM((PEER_PIECE, tm), F32),
            pltpu.VMEM((PEER_PIECE, tm), F32),
            pltpu.VMEM((PEER_PIECE, tm), BF16),
            pltpu.VMEM((PEER_PIECE, tm), BF16),
        ],
        compiler_params=_cparams("arbitrary", "arbitrary"),
        name="peer_experts",
    )(h2, s1, s2, rout, u_bf, vt_bf, x, gate, final_g)


def _peer(x, seg_fn, span, norm_g, shift, scale, gate, wq_t, keys, u_bf, vt_bf, final_g, final_norm):
    tm_r = min(512, span)
    tm_e = min(512, span)
    h2, s1, s2, rout = _peer_route(x, lambda i: seg_fn(i, tm_r), tm_r, norm_g, shift, scale, wq_t, keys)
    return _peer_experts(h2, s1, s2, rout, u_bf, vt_bf, x, lambda i: seg_fn(i, tm_e), tm_e, gate,
                         final_g, final_norm)


def _rope_tables(n_lat):
    rows = n_lat // GRID_W
    row = jnp.broadcast_to(jnp.arange(rows)[:, None], (rows, GRID_W)).reshape(-1).astype(F32)
    col = jnp.broadcast_to(jnp.arange(GRID_W)[None, :], (rows, GRID_W)).reshape(-1).astype(F32)
    inv = ROPE_BASE ** (-jnp.arange(0, ROPE_AXIS_DIM, 2, dtype=F32) / ROPE_AXIS_DIM)
    ang_r = row[:, None] * inv
    ang_c = col[:, None] * inv
    cr, sr, cc, sc = jnp.cos(ang_r), jnp.sin(ang_r), jnp.cos(ang_c), jnp.sin(ang_c)
    cos64 = jnp.concatenate([cr, cr, cc, cc], axis=-1)
    sin64 = jnp.concatenate([-sr, sr, -sc, sc], axis=-1)
    reps = LANES // SWA_HEAD_DIM
    return jnp.tile(cos64, (1, reps)), jnp.tile(sin64, (1, reps))


def _swa_weight_layout(w_in):
    qw = SWA_Q_HEADS * SWA_HEAD_DIM
    kvw = SWA_KV_HEADS * SWA_HEAD_DIM
    half = ROPE_AXIS_DIM // 2
    q_cols = np.arange(qw)
    q_swap = (q_cols // SWA_HEAD_DIM) * SWA_HEAD_DIM + ((q_cols % SWA_HEAD_DIM) ^ half)
    dup = np.arange(SWA_KV_HEADS * 2 * SWA_HEAD_DIM)
    kv_dup = (dup // (2 * SWA_HEAD_DIM)) * SWA_HEAD_DIM + (dup % SWA_HEAD_DIM)
    kv_dup_swap = (kv_dup // SWA_HEAD_DIM) * SWA_HEAD_DIM + ((kv_dup % SWA_HEAD_DIM) ^ half)
    wq = w_in[:, :qw]
    wk = w_in[:, qw:qw + kvw]
    wv = w_in[:, qw + kvw:]
    w_cat = jnp.concatenate([wq, wq[:, q_swap], wk[:, kv_dup], wk[:, kv_dup_swap], wv[:, kv_dup]], axis=1)
    w_kv = jnp.concatenate([wk[:, kv_dup], wv[:, kv_dup]], axis=1)
    return w_cat.astype(BF16), w_kv.astype(BF16)


def kernel(x, c, ctx, c_ctx, ada_w, ada_b, norm1_g, norm2_g, gdn_w_in, gdn_conv_w, gdn_a_log, gdn_dt_bias,
           gdn_norm_g, gdn_w_out, swa_w_in, swa_sinks, swa_w_out, peer_w_query, peer_sub_keys, peer_u, peer_v,
           final_g):
    return _forward(x, c, ctx, c_ctx, ada_w, ada_b, norm1_g, norm2_g, gdn_w_in, gdn_conv_w, gdn_a_log,
                    gdn_dt_bias, gdn_norm_g, gdn_w_out, swa_w_in, swa_sinks, swa_w_out, peer_w_query,
                    peer_sub_keys, peer_u, peer_v, final_g)[0]


def _forward(x, c, ctx, c_ctx, ada_w, ada_b, norm1_g, norm2_g, gdn_w_in, gdn_conv_w, gdn_a_log, gdn_dt_bias,
             gdn_norm_g, gdn_w_out, swa_w_in, swa_sinks, swa_w_out, peer_w_query, peer_sub_keys, peer_u, peer_v,
             final_g):
    b, l, d = x.shape
    nctx = ctx.shape[1]
    depth = ada_w.shape[0]
    assert depth == 2 and l % 256 == 0 and nctx % 128 == 0 and b + 1 <= 16

    mod_rows = 16
    cmat = jnp.zeros((mod_rows, d), F32).at[:b].set(c).at[b].set(c_ctx)
    ada = _ada_params(cmat, ada_w, ada_b)
    mods = ada.reshape(depth, mod_rows, ADA_CHUNKS, d).transpose(0, 2, 1, 3)[:, :, :, None, :]

    xl = x.reshape(b * l, d)
    xc = ctx.reshape(b * nctx, d)

    def lat_seg(i, tm):
        return i // (l // tm)

    def ctx_seg(i, tm):
        return b

    streams = ((lat_seg, l), (ctx_seg, nctx))

    def peer_weights(li):
        wq_t = peer_w_query[li].T.astype(BF16)
        keys = peer_sub_keys[li].reshape(PEER_HEADS * 2, N_KEYS, -1).astype(BF16)
        u_p = peer_u[li].astype(BF16).reshape(-1, PEER_PIECE, d)
        vt_p = peer_v[li].astype(BF16).reshape(-1, PEER_PIECE, d).transpose(0, 2, 1)
        return wq_t, keys, u_p, vt_p

    final_row = final_g.reshape(1, d)

    sh1, sc1, g1, sh2, sc2, g2 = (mods[0, j] for j in range(ADA_CHUNKS))
    n1 = norm1_g[0].reshape(1, d)
    n2 = norm2_g[0].reshape(1, d)
    width = GDN_HEADS * GDN_DK
    w_in = gdn_w_in[0]
    w_main = w_in[:, :4 * width].astype(BF16)
    w_abt = w_in[:, 4 * width:].T
    conv_w8 = jnp.zeros((8, 3 * width), F32).at[:GDN_CONV].set(gdn_conv_w[0])
    alog_rows = jnp.broadcast_to(gdn_a_log[0].reshape(2 * GDN_HEADS, 1), (2 * GDN_HEADS, LANES))
    dt_rows = jnp.broadcast_to(gdn_dt_bias[0].reshape(2 * GDN_HEADS, 1), (2 * GDN_HEADS, LANES))
    gdn_ng = gdn_norm_g[0].reshape(1, GDN_DK)
    w_out0 = gdn_w_out[0].astype(BF16)

    tok = {}
    for name, xs, (seg, slen) in (("ctx", xc, streams[1]), ("lat", xl, streams[0])):
        tm = min(512, slen)
        p_main, ab_t = _gdn_inproj(xs, lambda i: seg(i, tm), tm, n1, sh1, sc1, w_main, w_abt)
        qh, kh, vh = _gdn_prep(p_main, b, slen, conv_w8)
        tok[name] = (xs, seg, slen, tm, p_main, ab_t, qh, kh, vh)

    s0 = jnp.zeros((b, 2, GDN_HEADS, GDN_DK, GDN_DK), F32)
    new_x = {}
    for name in ("ctx", "lat"):
        xs, seg, slen, tm, p_main, ab_t, qh, kh, vh = tok[name]
        o_f, o_b, s0 = _gdn_scan(qh, kh, vh, ab_t, b, slen, alog_rows, dt_rows, s0)
        new_x[name] = _gdn_out(o_f, o_b, p_main, xs, lambda i: seg(i, tm), tm, g1, gdn_ng, w_out0)

    wq_t, keys, u_bf, vt_bf = peer_weights(0)
    xl = _peer(new_x["lat"], lat_seg, l, n2, sh2, sc2, g2, wq_t, keys, u_bf, vt_bf, final_row, False)
    xc = _peer(new_x["ctx"], ctx_seg, b * nctx, n2, sh2, sc2, g2, wq_t, keys, u_bf, vt_bf, final_row, False)
    streams_seen = dict(x1=new_x["lat"], xc1=new_x["ctx"], x2=xl, xc2=xc)

    sh1, sc1, g1, sh2, sc2, g2 = (mods[1, j] for j in range(ADA_CHUNKS))
    n1 = norm1_g[1].reshape(1, d)
    n2 = norm2_g[1].reshape(1, d)
    w_cat, w_kv = _swa_weight_layout(swa_w_in[0])
    cos_t, sin_t = _rope_tables(l)
    tm = min(512, l)
    q, k2, v2 = _swa_inproj(xl, lambda i: lat_seg(i, tm), tm, n1, sh1, sc1, w_cat, cos_t, sin_t)
    tmc = min(512, nctx)
    k2c, v2c = _swa_ctx_kv(xc, lambda i: ctx_seg(i, tmc), tmc, n1, sh1, sc1, w_kv)
    sink_rows = jnp.broadcast_to(swa_sinks[0].reshape(SWA_Q_HEADS, 1), (SWA_Q_HEADS, LANES))
    att = _attention(q, k2, v2, k2c, v2c, sink_rows, b, l, nctx)
    xl = _proj_residual(att, xl, lambda i: lat_seg(i, tm), tm, g1, swa_w_out[0].astype(BF16))
    streams_seen["x3"] = xl
    wq_t, keys, u_bf, vt_bf = peer_weights(1)
    out = _peer(xl, lat_seg, l, n2, sh2, sc2, g2, wq_t, keys, u_bf, vt_bf, final_row, True)
    return out.reshape(b, l, d), streams_seen
```

```python
import functools
import math

import numpy as np
import jax
import jax.numpy as jnp
from jax import lax
from jax.experimental import pallas as pl
from jax.experimental.pallas import tpu as pltpu

F32 = jnp.float32
BF16 = jnp.bfloat16

GRID_W = 64
GDN_HEADS = 8
GDN_DK = 128
GDN_CONV = 5
SWA_Q_HEADS = 16
SWA_KV_HEADS = 4
SWA_HEAD_DIM = 64
SWA_GROUP = SWA_Q_HEADS // SWA_KV_HEADS
ROPE_BASE = 10000.0
ROPE_AXIS_DIM = SWA_HEAD_DIM // 2
PEER_HEADS = 8
N_KEYS = 128
PEER_TOPK = 16
ADA_CHUNKS = 6
RMS_EPS = 1e-6
NEG_INF = -1e30

LANES = 128
SUBLANES_BF16 = 16
VMEM_LIMIT_BYTES = 60 * 1024 * 1024

CHUNK = LANES
ATTN_BLOCK = 128
ATTN_GROUPS_PER_BATCH = 2
EXPERT_CHUNK = 4096
PEER_PIECE = 256
ADA_TN = 1024
LOWEST = -3.0e38
INV_SQRT2 = 0.7071067811865476


def _cparams(*sem):
    return pltpu.CompilerParams(dimension_semantics=sem, vmem_limit_bytes=VMEM_LIMIT_BYTES)


def _dot(a, b):
    return jnp.dot(a, b, preferred_element_type=F32)


def _dot_nt(a, b):
    return lax.dot_general(a, b, (((1,), (1,)), ((), ())), preferred_element_type=F32)


def _dot_tn(a, b):
    return lax.dot_general(a, b, (((0,), (0,)), ((), ())), preferred_element_type=F32)


def _split2(a):
    hi = a.astype(BF16)
    lo = (a - hi.astype(F32)).astype(BF16)
    return hi, lo


def _split3(a):
    a1 = a.astype(BF16)
    r1 = a - a1.astype(F32)
    a2 = r1.astype(BF16)
    a3 = (r1 - a2.astype(F32)).astype(BF16)
    return a1, a2, a3


def _sigmoid(x):
    return 1.0 / (1.0 + jnp.exp(-x))


def _softplus(x):
    return jnp.maximum(x, 0.0) + jnp.log1p(jnp.exp(-jnp.abs(x)))


def _modulated(x, g, shift, scale):
    ms = jnp.mean(x * x, axis=-1, keepdims=True)
    y = x * lax.rsqrt(ms + RMS_EPS)
    return (y * g) * (1.0 + scale) + shift


def _ada_kernel(c_ref, w_ref, b_ref, o_ref):
    c = c_ref[...]
    s = c * _sigmoid(c)
    sh, sl = _split2(s)
    wh, wl = _split2(w_ref[0])
    o_ref[0] = _dot(sh, wh) + _dot(sh, wl) + _dot(sl, wh) + b_ref[0]


def _ada_params(cmat, ada_w, ada_b):
    depth, d, n = ada_w.shape
    rows = cmat.shape[0]
    return pl.pallas_call(
        _ada_kernel,
        grid=(depth, n // ADA_TN),
        in_specs=[
            pl.BlockSpec((rows, d), lambda l, j: (0, 0)),
            pl.BlockSpec((1, d, ADA_TN), lambda l, j: (l, 0, j)),
            pl.BlockSpec((1, 1, ADA_TN), lambda l, j: (l, 0, j)),
        ],
        out_specs=pl.BlockSpec((1, rows, ADA_TN), lambda l, j: (l, 0, j)),
        out_shape=jax.ShapeDtypeStruct((depth, rows, n), F32),
        compiler_params=_cparams("arbitrary", "arbitrary"),
        name="ada_params",
    )(cmat, ada_w, ada_b.reshape(depth, 1, n))


def _gdn_inproj_kernel(x_ref, g_ref, sh_ref, sc_ref, w_ref, wab_ref, p_ref, ab_ref):
    h = _modulated(x_ref[...], g_ref[...], sh_ref[0], sc_ref[0])
    hb = h.astype(BF16)
    p_ref[...] = _dot(hb, w_ref[...]).astype(BF16)
    hl = (h - hb.astype(F32)).astype(BF16)
    wh, wl = _split2(wab_ref[...])
    ab_ref[...] = _dot_nt(wh, hb) + _dot_nt(wh, hl) + _dot_nt(wl, hb)


def _gdn_inproj(x, seg_fn, tm, norm_g, shift, scale, w_main, w_abt):
    t, d = x.shape
    n = w_main.shape[1]
    na = w_abt.shape[0]
    return pl.pallas_call(
        _gdn_inproj_kernel,
        grid=(t // tm,),
        in_specs=[
            pl.BlockSpec((tm, d), lambda i: (i, 0)),
            pl.BlockSpec((1, d), lambda i: (0, 0)),
            pl.BlockSpec((1, 1, d), lambda i: (seg_fn(i), 0, 0)),
            pl.BlockSpec((1, 1, d), lambda i: (seg_fn(i), 0, 0)),
            pl.BlockSpec((d, n), lambda i: (0, 0)),
            pl.BlockSpec((na, d), lambda i: (0, 0)),
        ],
        out_specs=[
            pl.BlockSpec((tm, n), lambda i: (i, 0)),
            pl.BlockSpec((na, tm), lambda i: (0, i)),
        ],
        out_shape=[jax.ShapeDtypeStruct((t, n), BF16), jax.ShapeDtypeStruct((na, t), F32)],
        compiler_params=_cparams("arbitrary"),
        name="gdn_inproj",
    )(x, norm_g, shift, scale, w_main, w_abt)


CONV_HALO = SUBLANES_BF16
CONV_PAD = GDN_CONV // 2


def _gdn_prep_kernel(main_ref, prev_ref, next_ref, w_ref, q_ref, k_ref, v_ref, ext_scr):
    t = pl.program_id(1)
    nt = pl.num_programs(1)
    tl = main_ref.shape[0]
    base = CONV_HALO
    ext_scr[base:base + tl, :] = main_ref[...].astype(F32)
    prev = prev_ref[...].astype(F32)[CONV_HALO - CONV_PAD:, :]
    nxt = next_ref[...].astype(F32)[:CONV_PAD, :]
    ext_scr[base - CONV_PAD:base, :] = prev * (t > 0).astype(F32)
    ext_scr[base + tl:base + tl + CONV_PAD, :] = nxt * (t < nt - 1).astype(F32)
    nblk = main_ref.shape[1] // LANES
    for cb in range(nblk):
        cols = slice(cb * LANES, (cb + 1) * LANES)
        acc = jnp.zeros((tl, LANES), F32)
        for j in range(GDN_CONV):
            acc = acc + w_ref[j:j + 1, cols] * ext_scr[base - CONV_PAD + j:base - CONV_PAD + j + tl, cols]
        y = acc * _sigmoid(acc)
        which, head = divmod(cb, GDN_HEADS)
        if which < 2:
            y = y * lax.rsqrt(jnp.sum(y * y, axis=-1, keepdims=True) + 1e-6)
            if which == 0:
                y = y * (GDN_DK ** -0.5)
        (q_ref, k_ref, v_ref)[which][head] = y.astype(BF16)


def _gdn_prep(p_main, nseq, seqlen, conv_w8):
    t = p_main.shape[0]
    tl = min(256, seqlen)
    tps = seqlen // tl
    ncol = 3 * GDN_HEADS * GDN_DK
    hb = tl // CONV_HALO
    last_halo = t // CONV_HALO - 1
    out = jax.ShapeDtypeStruct((GDN_HEADS, t, GDN_DK), BF16)
    ospec = pl.BlockSpec((GDN_HEADS, tl, GDN_DK), lambda s, i: (0, s * tps + i, 0))
    return pl.pallas_call(
        _gdn_prep_kernel,
        grid=(nseq, tps),
        in_specs=[
            pl.BlockSpec((tl, ncol), lambda s, i: (s * tps + i, 0)),
            pl.BlockSpec((CONV_HALO, ncol), lambda s, i: (jnp.maximum((s * tps + i) * hb - 1, 0), 0)),
            pl.BlockSpec((CONV_HALO, ncol), lambda s, i: (jnp.minimum((s * tps + i + 1) * hb, last_halo), 0)),
            pl.BlockSpec((8, ncol), lambda s, i: (0, 0)),
        ],
        out_specs=[ospec, ospec, ospec],
        out_shape=[out, out, out],
        scratch_shapes=[pltpu.VMEM((tl + 2 * CONV_HALO, ncol), F32)],
        compiler_params=_cparams("arbitrary", "arbitrary"),
        name="gdn_prep",
    )(p_main, p_main, p_main, conv_w8)


INV_BASE = 16


def _each(fn, *lists):
    return [fn(*args) for args in zip(*lists)]


def _unit_triangular_inverses(lmats, ri, ci):
    c = lmats[0].shape[0]
    eye = jnp.where(ri == ci, 1.0, 0.0)
    same_base = (ri // INV_BASE) == (ci // INV_BASE)
    a = [jnp.where(same_base, -l, 0.0) for l in lmats]
    t = [eye + x for x in a]
    ap = [x.astype(BF16) for x in a]
    for _ in range(1, int(math.log2(INV_BASE))):
        ap = [_dot(x, x).astype(BF16) for x in ap]
        t = _each(lambda tt, x: tt + _dot(tt.astype(BF16), x), t, ap)
    size = INV_BASE
    while size < c:
        same_small = (ri // size) == (ci // size)
        same_big = (ri // (2 * size)) == (ci // (2 * size))
        couple = jnp.logical_and(same_big, jnp.logical_not(same_small))
        l1 = [jnp.where(couple, l, 0.0).astype(BF16) for l in lmats]
        tb = [x.astype(BF16) for x in t]
        mid = _each(lambda l, x: _dot(l, x).astype(BF16), l1, tb)
        t = _each(lambda tt, x, m: tt - _dot(x, m), t, tb, mid)
        size *= 2
    return t


def _delta_chunks(q, k, v, gc_row, beta_row, tot_row, s_prev, lower):
    c = q[0].shape[0]
    dk = k[0].shape[1]
    ri = lax.broadcasted_iota(jnp.int32, (c, c), 0)
    ci = lax.broadcasted_iota(jnp.int32, (c, c), 1)
    incl = [ri >= ci if lo else ri <= ci for lo in lower]
    strict = [ri > ci if lo else ri < ci for lo in lower]
    gc_rb = [jnp.broadcast_to(g, (c, c)) for g in gc_row]
    gc_cb = [g.T for g in gc_rb]
    beta_cb = [jnp.broadcast_to(b, (c, c)).T for b in beta_row]
    tot_b = [jnp.broadcast_to(t, (c, c)) for t in tot_row]
    decay = _each(lambda m, gc, gr: jnp.where(m, jnp.exp(jnp.where(m, gc - gr, 0.0)), 0.0), incl, gc_cb, gc_rb)
    kf = [x.astype(F32) for x in k]
    kk = _each(_dot_nt, k, k)
    qk = _each(_dot_nt, q, k)
    lmat = _each(lambda x, b, m, d: (x * b) * jnp.where(m, d, 0.0), kk, beta_cb, strict, decay)
    aqk = _each(lambda x, d: (x * d).astype(BF16), qk, decay)
    eg_cb = [jnp.exp(g) for g in gc_cb]
    rhs = _each(lambda vv, kx, b, e: jnp.concatenate([vv.astype(F32) * b, kx * (b * e)], axis=1).astype(BF16),
                v, kf, beta_cb, eg_cb)
    t_inv = _unit_triangular_inverses(lmat, ri, ci)
    x = _each(lambda t, r: _dot(t.astype(BF16), r), t_inv, rhs)
    sb = [s.astype(BF16) for s in s_prev]
    v_new = _each(lambda xx, s: xx[:, :dk] - _dot(xx[:, dk:].astype(BF16), s), x, sb)
    vb = [vn.astype(BF16) for vn in v_new]
    qd = _each(lambda qq, e: (qq.astype(F32) * e).astype(BF16), q, eg_cb)
    o = _each(lambda a, s, w, vv: _dot(a, s) + _dot(w, vv), qd, sb, aqk, vb)
    kd = _each(lambda kx, t, g: (kx * jnp.exp(t - g)).astype(BF16), kf, tot_b, gc_cb)
    s_next = _each(lambda s, t, a, vv: s * jnp.exp(t) + _dot_tn(a, vv), s_prev, tot_b, kd, vb)
    return o, s_next


def _gdn_scan_kernel(qf_ref, kf_ref, vf_ref, qb_ref, kb_ref, vb_ref, abf_ref, abb_ref, alog_ref, dt_ref,
                     s0_ref, of_ref, ob_ref, sfin_ref, s_scr):
    i = pl.program_id(1)
    n = pl.num_programs(1)
    nh = GDN_HEADS

    @pl.when(i == 0)
    def _():
        s_scr[...] = s0_ref[0]

    c = CHUNK
    ri = lax.broadcasted_iota(jnp.int32, (c, c), 0)
    ci = lax.broadcasted_iota(jnp.int32, (c, c), 1)
    one = jnp.ones((c, c), BF16)
    upper_incl = jnp.where(ri <= ci, 1.0, 0.0).astype(BF16)
    lower_incl = jnp.where(ri >= ci, 1.0, 0.0).astype(BF16)

    def cums(g, tri):
        g1, g2, g3 = _split3(g)
        return (_dot(g1, tri) + _dot(g2, tri) + _dot(g3, tri),
                _dot(g1, one) + _dot(g2, one) + _dot(g3, one))

    g_f = -jnp.exp(alog_ref[0:nh, :]) * _softplus(abf_ref[0:nh, :] + dt_ref[0:nh, :])
    g_b = -jnp.exp(alog_ref[nh:2 * nh, :]) * _softplus(abb_ref[nh:2 * nh, :] + dt_ref[nh:2 * nh, :])
    gc_f, tot_f = cums(g_f, upper_incl)
    gc_b, tot_b = cums(g_b, lower_incl)
    rows = jnp.concatenate([gc_f, gc_b, _sigmoid(abf_ref[2 * nh:3 * nh, :]), _sigmoid(abb_ref[3 * nh:4 * nh, :]),
                            tot_f, tot_b], axis=0)

    row = lambda r: rows[r:r + 1]
    heads = range(nh)
    o, s_next = _delta_chunks(
        q=[qf_ref[h] for h in heads] + [qb_ref[h] for h in heads],
        k=[kf_ref[h] for h in heads] + [kb_ref[h] for h in heads],
        v=[vf_ref[h] for h in heads] + [vb_ref[h] for h in heads],
        gc_row=[row(h) for h in heads] + [row(nh + h) for h in heads],
        beta_row=[row(2 * nh + h) for h in heads] + [row(3 * nh + h) for h in heads],
        tot_row=[row(4 * nh + h) for h in heads] + [row(5 * nh + h) for h in heads],
        s_prev=[s_scr[0, h] for h in heads] + [s_scr[1, h] for h in heads],
        lower=[True] * nh + [False] * nh)
    for h in heads:
        s_scr[0, h] = s_next[h]
        s_scr[1, h] = s_next[nh + h]
        of_ref[h] = o[h].astype(BF16)
        ob_ref[h] = o[nh + h].astype(BF16)

    @pl.when(i == n - 1)
    def _():
        sfin_ref[0] = s_scr[...]


def _gdn_scan(qh, kh, vh, ab_t, nseq, seqlen, alog_rows, dt_rows, s0):
    t = qh.shape[1]
    nt = seqlen // CHUNK
    nab = ab_t.shape[0]
    hspec_f = pl.BlockSpec((GDN_HEADS, CHUNK, GDN_DK), lambda s, i: (0, s * nt + i, 0))
    hspec_b = pl.BlockSpec((GDN_HEADS, CHUNK, GDN_DK), lambda s, i: (0, s * nt + nt - 1 - i, 0))
    sspec = pl.BlockSpec((1, 2, GDN_HEADS, GDN_DK, GDN_DK), lambda s, i: (s, 0, 0, 0, 0))
    oshape = jax.ShapeDtypeStruct((GDN_HEADS, t, GDN_DK), BF16)
    return pl.pallas_call(
        _gdn_scan_kernel,
        grid=(nseq, nt),
        in_specs=[
            hspec_f, hspec_f, hspec_f, hspec_b, hspec_b, hspec_b,
            pl.BlockSpec((nab, CHUNK), lambda s, i: (0, s * nt + i)),
            pl.BlockSpec((nab, CHUNK), lambda s, i: (0, s * nt + nt - 1 - i)),
            pl.BlockSpec((2 * GDN_HEADS, LANES), lambda s, i: (0, 0)),
            pl.BlockSpec((2 * GDN_HEADS, LANES), lambda s, i: (0, 0)),
            sspec,
        ],
        out_specs=[hspec_f, hspec_b, sspec],
        out_shape=[oshape, oshape, jax.ShapeDtypeStruct(s0.shape, F32)],
        scratch_shapes=[
            pltpu.VMEM((2, GDN_HEADS, GDN_DK, GDN_DK), F32),
        ],
        compiler_params=_cparams("arbitrary", "arbitrary"),
        name="gdn_scan",
    )(qh, kh, vh, qh, kh, vh, ab_t, ab_t, alog_rows, dt_rows, s0)


def _gdn_out_kernel(of_ref, ob_ref, z_ref, x_ref, gate_ref, ng_ref, w_ref, o_ref):
    parts = []
    for h in range(GDN_HEADS):
        o = of_ref[h].astype(F32) + ob_ref[h].astype(F32)
        on = o * lax.rsqrt(jnp.mean(o * o, axis=-1, keepdims=True) + RMS_EPS) * ng_ref[...]
        z = z_ref[:, h * GDN_DK:(h + 1) * GDN_DK].astype(F32)
        parts.append((on * (z * _sigmoid(z))).astype(BF16))
    a = jnp.concatenate(parts, axis=-1)
    o_ref[...] = x_ref[...] + gate_ref[0] * _dot(a, w_ref[...])


def _gdn_out(o_f, o_b, p_main, x, seg_fn, tm, gate, norm_g, w_out):
    t, d = x.shape
    width = GDN_HEADS * GDN_DK
    zblk = 3
    hspec = pl.BlockSpec((GDN_HEADS, tm, GDN_DK), lambda i: (0, i, 0))
    return pl.pallas_call(
        _gdn_out_kernel,
        grid=(t // tm,),
        in_specs=[
            hspec, hspec,
            pl.BlockSpec((tm, width), lambda i: (i, zblk)),
            pl.BlockSpec((tm, d), lambda i: (i, 0)),
            pl.BlockSpec((1, 1, d), lambda i: (seg_fn(i), 0, 0)),
            pl.BlockSpec((1, GDN_DK), lambda i: (0, 0)),
            pl.BlockSpec((width, d), lambda i: (0, 0)),
        ],
        out_specs=pl.BlockSpec((tm, d), lambda i: (i, 0)),
        out_shape=jax.ShapeDtypeStruct((t, d), F32),
        compiler_params=_cparams("arbitrary"),
        name="gdn_out",
    )(o_f, o_b, p_main, x, gate, norm_g, w_out)


def _swa_inproj_kernel(x_ref, g_ref, sh_ref, sc_ref, w_ref, cos_ref, sin_ref, q_ref, k_ref, v_ref):
    h = _modulated(x_ref[...], g_ref[...], sh_ref[0], sc_ref[0]).astype(BF16)
    p = _dot(h, w_ref[...])
    qw = SWA_Q_HEADS * SWA_HEAD_DIM
    kw = SWA_KV_HEADS * 2 * SWA_HEAD_DIM
    cos = cos_ref[...]
    sin = sin_ref[...]
    cq = jnp.concatenate([cos] * (qw // LANES), axis=-1)
    sq = jnp.concatenate([sin] * (qw // LANES), axis=-1)
    ck = jnp.concatenate([cos] * (kw // LANES), axis=-1)
    sk = jnp.concatenate([sin] * (kw // LANES), axis=-1)
    o = 0
    q = p[:, o:o + qw] * cq + p[:, o + qw:o + 2 * qw] * sq
    o += 2 * qw
    k = p[:, o:o + kw] * ck + p[:, o + kw:o + 2 * kw] * sk
    o += 2 * kw
    q_ref[...] = (q * (SWA_HEAD_DIM ** -0.5)).astype(BF16)
    k_ref[...] = k.astype(BF16)
    v_ref[...] = p[:, o:o + kw].astype(BF16)


def _swa_inproj(x, seg_fn, tm, norm_g, shift, scale, w_cat, cos_t, sin_t):
    t, d = x.shape
    n = w_cat.shape[1]
    qw = SWA_Q_HEADS * SWA_HEAD_DIM
    kw = SWA_KV_HEADS * 2 * SWA_HEAD_DIM
    tps = cos_t.shape[0] // tm
    return pl.pallas_call(
        _swa_inproj_kernel,
        grid=(t // tm,),
        in_specs=[
            pl.BlockSpec((tm, d), lambda i: (i, 0)),
            pl.BlockSpec((1, d), lambda i: (0, 0)),
            pl.BlockSpec((1, 1, d), lambda i: (seg_fn(i), 0, 0)),
            pl.BlockSpec((1, 1, d), lambda i: (seg_fn(i), 0, 0)),
            pl.BlockSpec((d, n), lambda i: (0, 0)),
            pl.BlockSpec((tm, LANES), lambda i: (i % tps, 0)),
            pl.BlockSpec((tm, LANES), lambda i: (i % tps, 0)),
        ],
        out_specs=[
            pl.BlockSpec((tm, qw), lambda i: (i, 0)),
            pl.BlockSpec((tm, kw), lambda i: (i, 0)),
            pl.BlockSpec((tm, kw), lambda i: (i, 0)),
        ],
        out_shape=[jax.ShapeDtypeStruct((t, qw), BF16), jax.ShapeDtypeStruct((t, kw), BF16),
                   jax.ShapeDtypeStruct((t, kw), BF16)],
        compiler_params=_cparams("arbitrary"),
        name="swa_inproj",
    )(x, norm_g, shift, scale, w_cat, cos_t, sin_t)


def _swa_ctx_kv_kernel(x_ref, g_ref, sh_ref, sc_ref, w_ref, k_ref, v_ref):
    h = _modulated(x_ref[...], g_ref[...], sh_ref[0], sc_ref[0]).astype(BF16)
    p = _dot(h, w_ref[...])
    kw = k_ref.shape[1]
    k_ref[...] = p[:, :kw].astype(BF16)
    v_ref[...] = p[:, kw:].astype(BF16)


def _swa_ctx_kv(xc, seg_fn, tm, norm_g, shift, scale, w_kv):
    t, d = xc.shape
    kw = SWA_KV_HEADS * 2 * SWA_HEAD_DIM
    return pl.pallas_call(
        _swa_ctx_kv_kernel,
        grid=(t // tm,),
        in_specs=[
            pl.BlockSpec((tm, d), lambda i: (i, 0)),
            pl.BlockSpec((1, d), lambda i: (0, 0)),
            pl.BlockSpec((1, 1, d), lambda i: (seg_fn(i), 0, 0)),
            pl.BlockSpec((1, 1, d), lambda i: (seg_fn(i), 0, 0)),
            pl.BlockSpec((d, 2 * kw), lambda i: (0, 0)),
        ],
        out_specs=[pl.BlockSpec((tm, kw), lambda i: (i, 0)), pl.BlockSpec((tm, kw), lambda i: (i, 0))],
        out_shape=[jax.ShapeDtypeStruct((t, kw), BF16), jax.ShapeDtypeStruct((t, kw), BF16)],
        compiler_params=_cparams("arbitrary"),
        name="swa_ctx_kv",
    )(xc, norm_g, shift, scale, w_kv)


def _attn_kernel(q_ref, kp_ref, kc_ref, kn_ref, vp_ref, vc_ref, vn_ref, kx_ref, vx_ref, sink_ref, o_ref):
    n = pl.program_id(1)
    nb = pl.num_programs(1)
    blk = ATTN_BLOCK
    nctx = kx_ref.shape[0]
    ri = lax.broadcasted_iota(jnp.int32, (blk, blk), 0)
    ci = lax.broadcasted_iota(jnp.int32, (blk, blk), 1)
    never = 2 * blk
    ok_prev = (ci - ri) >= jnp.where(n > 0, 0, never)
    ok_next = (ri - ci) >= jnp.where(n < nb - 1, 0, never)
    lane = lax.broadcasted_iota(jnp.int32, (1, LANES), 1)
    lo_half = lane < SWA_HEAD_DIM
    keep_lo = jnp.where(lo_half, 1.0, 0.0).astype(BF16)
    keep_hi = jnp.where(lo_half, 0.0, 1.0).astype(BF16)
    gw = 2 * SWA_HEAD_DIM
    for g0 in range(0, SWA_KV_HEADS, ATTN_GROUPS_PER_BATCH):
        qps, kms, vcats, sinks, cols = [], [], [], [], []
        for g in range(g0, g0 + ATTN_GROUPS_PER_BATCH):
            gs = slice(g * gw, (g + 1) * gw)
            kcat = jnp.concatenate([kp_ref[:, gs], kc_ref[:, gs], kn_ref[:, gs], kx_ref[:, gs]], axis=0)
            vcat = jnp.concatenate([vp_ref[:, gs], vc_ref[:, gs], vn_ref[:, gs], vx_ref[:, gs]], axis=0)
            k_halves = (kcat * keep_lo, kcat * keep_hi)
            for pr in range(SWA_GROUP // 2):
                col = (g * (SWA_GROUP // 2) + pr) * LANES
                qp = q_ref[:, col:col + LANES]
                for half in range(2):
                    head = g * SWA_GROUP + pr * 2 + half
                    qps.append(qp)
                    kms.append(k_halves[half])
                    vcats.append(vcat)
                    sinks.append(jnp.broadcast_to(sink_ref[head:head + 1, :], (blk, LANES))[:, 0:1])
                    cols.append(col)
        s = _each(_dot_nt, qps, kms)
        s = [jnp.concatenate([jnp.where(ok_prev, x[:, :blk], NEG_INF), x[:, blk:2 * blk],
                              jnp.where(ok_next, x[:, 2 * blk:3 * blk], NEG_INF), x[:, 3 * blk:]], axis=1) for x in s]
        m = _each(lambda x, sk: jnp.maximum(jnp.max(x, axis=-1, keepdims=True), sk), s, sinks)
        p = _each(lambda x, mm: jnp.exp(x - mm), s, m)
        den = _each(lambda x, sk, mm: jnp.sum(x, axis=-1, keepdims=True) + jnp.exp(sk - mm), p, sinks, m)
        o = _each(lambda x, vv, dd: _dot(x.astype(BF16), vv) / dd, p, vcats, den)
        for j in range(0, len(o), 2):
            o_ref[:, cols[j]:cols[j] + LANES] = jnp.where(lo_half, o[j], o[j + 1]).astype(BF16)


def _attention(q, k2, v2, k2c, v2c, sink_rows, nseq, seqlen, nctx):
    t, qw = q.shape
    kw = k2.shape[1]
    blk = ATTN_BLOCK
    nb = seqlen // blk
    kv_prev = pl.BlockSpec((blk, kw), lambda s, i: (s * nb + jnp.maximum(i - 1, 0), 0))
    kv_cur = pl.BlockSpec((blk, kw), lambda s, i: (s * nb + i, 0))
    kv_next = pl.BlockSpec((blk, kw), lambda s, i: (s * nb + jnp.minimum(i + 1, nb - 1), 0))
    kv_ctx = pl.BlockSpec((nctx, kw), lambda s, i: (s, 0))
    return pl.pallas_call(
        _attn_kernel,
        grid=(nseq, nb),
        in_specs=[
            pl.BlockSpec((blk, qw), lambda s, i: (s * nb + i, 0)),
            kv_prev, kv_cur, kv_next, kv_prev, kv_cur, kv_next, kv_ctx, kv_ctx,
            pl.BlockSpec(sink_rows.shape, lambda s, i: (0, 0)),
        ],
        out_specs=pl.BlockSpec((blk, qw), lambda s, i: (s * nb + i, 0)),
        out_shape=jax.ShapeDtypeStruct((t, qw), BF16),
        compiler_params=_cparams("arbitrary", "arbitrary"),
        name="swa_attention",
    )(q, k2, k2, k2, v2, v2, v2, k2c, v2c, sink_rows)


def _proj_residual_kernel(a_ref, x_ref, gate_ref, w_ref, o_ref):
    o_ref[...] = x_ref[...] + gate_ref[0] * _dot(a_ref[...], w_ref[...])


def _proj_residual(a, x, seg_fn, tm, gate, w):
    t, d = x.shape
    k = a.shape[1]
    return pl.pallas_call(
        _proj_residual_kernel,
        grid=(t // tm,),
        in_specs=[
            pl.BlockSpec((tm, k), lambda i: (i, 0)),
            pl.BlockSpec((tm, d), lambda i: (i, 0)),
            pl.BlockSpec((1, 1, d), lambda i: (seg_fn(i), 0, 0)),
            pl.BlockSpec((k, d), lambda i: (0, 0)),
        ],
        out_specs=pl.BlockSpec((tm, d), lambda i: (i, 0)),
        out_shape=jax.ShapeDtypeStruct((t, d), F32),
        compiler_params=_cparams("arbitrary"),
        name="proj_residual",
    )(a, x, gate, w)


N_RANKS = PEER_TOPK + 1
RANK_ROWS = 24


def _batcher_network(n):
    pairs = []

    def merge(lo, length, r):
        step = 2 * r
        if step < length:
            merge(lo, length, step)
            merge(lo + r, length, step)
            for i in range(lo + r, lo + length - r, step):
                pairs.append((i, i + r))
        else:
            pairs.append((lo, lo + r))

    def sort(lo, length):
        if length > 1:
            half = length // 2
            sort(lo, half)
            sort(lo + half, half)
            merge(lo, length, 1)

    sort(0, n)
    return pairs


_SORT_NETWORK = _batcher_network(N_KEYS // 8)


def _tree_max(vals):
    vals = list(vals)
    while len(vals) > 1:
        nxt = [jnp.maximum(vals[i], vals[i + 1]) for i in range(0, len(vals) - 1, 2)]
        if len(vals) % 2:
            nxt.append(vals[-1])
        vals = nxt
    return vals[0]


def _peer_route_kernel(x_ref, g_ref, sh_ref, sc_ref, wqt_ref, keys_ref,
                       h2_ref, s1_ref, s2_ref, rout_ref, a_scr, b_scr):
    tm = x_ref.shape[0]
    nh = PEER_HEADS
    hb = _modulated(x_ref[...], g_ref[...], sh_ref[0], sc_ref[0]).astype(BF16)
    h2_ref[...] = hb
    nlc = tm // LANES
    rows_per_head = 2 * N_KEYS

    def head_body(h):
        qt = _dot_nt(wqt_ref[h * rows_per_head:(h + 1) * rows_per_head, :], hb).astype(BF16)
        chains = []
        for p, (s_ref, top_scr) in enumerate(((s1_ref, a_scr), (s2_ref, b_scr))):
            hp = h * 2 + p
            s = _dot(keys_ref[hp], qt[p * N_KEYS:(p + 1) * N_KEYS, :])
            s_ref[h] = s
            for lc in range(nlc):
                ls = slice(lc * LANES, (lc + 1) * LANES)
                chains.append((top_scr, ls, s[:, ls]))
        lists = [[x[g * 8:(g + 1) * 8, :] for g in range(N_KEYS // 8)] for (_, _, x) in chains]
        for (i, j) in _SORT_NETWORK:
            for lst in lists:
                hi = jnp.maximum(lst[i], lst[j])
                lst[j] = jnp.minimum(lst[i], lst[j])
                lst[i] = hi
        for r in range(N_RANKS):
            depth = min(len(lists[0]), N_RANKS - r)
            for (top_scr, ls, _), lst in zip(chains, lists):
                m = jnp.max(lst[0], axis=0, keepdims=True)
                top_scr[h, r:r + 1, ls] = m
                popped = lst[0] == m
                for kq in range(depth - 1):
                    lst[kq] = jnp.where(popped, lst[kq + 1], lst[kq])
                lst[depth - 1] = jnp.where(popped, LOWEST, lst[depth - 1])

    for h in range(nh):
        head_body(h)

    for lc in range(nlc):
        ls = slice(lc * LANES, (lc + 1) * LANES)
        av = [jnp.concatenate([a_scr[h, r:r + 1, ls] for h in range(nh)], axis=0) for r in range(N_RANKS)]
        bv = [jnp.concatenate([b_scr[h, r:r + 1, ls] for h in range(nh)], axis=0) for r in range(N_RANKS)]
        groups = [[av[r] + bv[c] for c in range(N_RANKS // (r + 1))] for r in range(N_RANKS)]
        cands = [cv for grp in groups for cv in grp]
        top = groups[0][0]
        kth = []
        for it in range(N_RANKS):
            m = _tree_max([grp[0] for grp in groups])
            kth.append(m)
            for grp in groups:
                depth = min(len(grp), N_RANKS - it)
                popped = grp[0] == m
                for kq in range(depth - 1):
                    grp[kq] = jnp.where(popped, grp[kq + 1], grp[kq])
                grp[depth - 1] = jnp.where(popped, LOWEST, grp[depth - 1])
        tau = 0.5 * (kth[PEER_TOPK - 1] + kth[PEER_TOPK])
        z = jnp.zeros_like(top)
        for cv in cands:
            z = z + jnp.where(cv >= tau, jnp.exp(cv - top), 0.0)
        rout_ref[0 * nh:1 * nh, ls] = tau
        rout_ref[1 * nh:2 * nh, ls] = av[0]
        rout_ref[2 * nh:3 * nh, ls] = bv[0]
        rout_ref[3 * nh:4 * nh, ls] = 1.0 / z


def _peer_route(x, seg_fn, tm, norm_g, shift, scale, wq_t, keys):
    t, d = x.shape
    nq = wq_t.shape[0]
    nh = PEER_HEADS
    sshape = jax.ShapeDtypeStruct((nh, N_KEYS, t), F32)
    sspec = pl.BlockSpec((nh, N_KEYS, tm), lambda i: (0, 0, i))
    return pl.pallas_call(
        _peer_route_kernel,
        grid=(t // tm,),
        in_specs=[
            pl.BlockSpec((tm, d), lambda i: (i, 0)),
            pl.BlockSpec((1, d), lambda i: (0, 0)),
            pl.BlockSpec((1, 1, d), lambda i: (seg_fn(i), 0, 0)),
            pl.BlockSpec((1, 1, d), lambda i: (seg_fn(i), 0, 0)),
            pl.BlockSpec((nq, d), lambda i: (0, 0)),
            pl.BlockSpec(keys.shape, lambda i: (0, 0, 0)),
        ],
        out_specs=[
            pl.BlockSpec((tm, d), lambda i: (i, 0)),
            sspec, sspec,
            pl.BlockSpec((4 * nh, tm), lambda i: (0, i)),
        ],
        out_shape=[jax.ShapeDtypeStruct((t, d), BF16), sshape, sshape,
                   jax.ShapeDtypeStruct((4 * nh, t), F32)],
        scratch_shapes=[
            pltpu.VMEM((nh, RANK_ROWS, tm), F32),
            pltpu.VMEM((nh, RANK_ROWS, tm), F32),
        ],
        compiler_params=_cparams("arbitrary"),
        name="peer_route",
    )(x, norm_g, shift, scale, wq_t, keys)


PEER_LANE_CHUNK = 128


def _peer_expert_kernel(h2_ref, s1_ref, s2_ref, rout_ref, u_ref, vt_ref, x_ref, gate_ref, fg_ref,
                        o_ref, acc_scr, e2_scr, ht0_scr, ht1_scr, a0_scr, a1_scr, *, final_norm):
    ht_scr = (ht0_scr, ht1_scr)
    a_scr = (a0_scr, a1_scr)
    c = pl.program_id(1)
    nc = pl.num_programs(1)
    nh = PEER_HEADS
    tm = h2_ref.shape[0]
    per_step = EXPERT_CHUNK // N_KEYS

    @pl.when(c == 0)
    def _():
        acc_scr[...] = jnp.zeros_like(acc_scr)
        for h in range(nh):
            e2_scr[h] = (jnp.exp(s2_ref[h] - rout_ref[2 * nh + h:2 * nh + h + 1, :])
                         * rout_ref[3 * nh + h:3 * nh + h + 1, :])

    tau = rout_ref[0:nh, :]
    m1 = rout_ref[nh:2 * nh, :]
    m2 = rout_ref[2 * nh:3 * nh, :]
    inv_z = rout_ref[3 * nh:4 * nh, :]
    lcw = min(PEER_LANE_CHUNK, tm)
    n_pieces = u_ref.shape[0]
    keys_per_piece = PEER_PIECE // N_KEYS

    def hidden(p, slot):
        ht_scr[slot][...] = _dot_nt(u_ref[p], h2_ref[...])

    def gate_piece(p, slot):
        for kk in range(keys_per_piece):
            i = c * per_step + p * keys_per_piece + kk
            s1rows = jnp.concatenate([s1_ref[h, pl.ds(i, 1), :] for h in range(nh)], axis=0)
            thr = jnp.exp(tau - s1rows - m2) * inv_z
            coef = jnp.exp(s1rows - m1)
            erows = slice(kk * N_KEYS, (kk + 1) * N_KEYS)
            for lc in range(tm // lcw):
                ls = slice(lc * lcw, (lc + 1) * lcw)
                gate = jnp.zeros((N_KEYS, lcw), F32)
                for h in range(nh):
                    e2 = e2_scr[h, :, ls]
                    gate = gate + jnp.where(e2 >= thr[h:h + 1, ls], e2, 0.0) * coef[h:h + 1, ls]
                hh = ht_scr[slot][erows, ls]
                act = 0.5 * hh * (1.0 + lax.erf(hh * INV_SQRT2))
                a_scr[slot][erows, ls] = (act * gate).astype(BF16)

    def project(p, slot):
        acc_scr[...] += _dot(vt_ref[p], a_scr[slot][...])

    hidden(0, 0)
    hidden(1, 1)
    gate_piece(0, 0)

    def stage_pair(j, carry):
        p = 2 * j + 2
        project(p - 2, 0)
        hidden(p, 0)
        gate_piece(p - 1, 1)
        project(p - 1, 1)
        hidden(p + 1, 1)
        gate_piece(p, 0)
        return carry

    for pair in range((n_pieces - 2) // 2):
        stage_pair(pair, 0)
    gate_piece(n_pieces - 1, 1)
    project(n_pieces - 2, 0)
    project(n_pieces - 1, 1)

    @pl.when(c == nc - 1)
    def _():
        out = x_ref[...] + gate_ref[0] * acc_scr[...].T
        if final_norm:
            ms = jnp.mean(out * out, axis=-1, keepdims=True)
            out = out * lax.rsqrt(ms + RMS_EPS) * fg_ref[...]
        o_ref[...] = out


def _peer_experts(h2, s1, s2, rout, u_bf, vt_bf, x, seg_fn, tm, gate, final_g, final_norm):
    t, d = x.shape
    ppc = EXPERT_CHUNK // PEER_PIECE
    nh = PEER_HEADS
    sspec = pl.BlockSpec((nh, N_KEYS, tm), lambda i, c: (0, 0, i))
    return pl.pallas_call(
        functools.partial(_peer_expert_kernel, final_norm=final_norm),
        grid=(t // tm, u_bf.shape[0] // ppc),
        in_specs=[
            pl.BlockSpec((tm, d), lambda i, c: (i, 0)),
            sspec, sspec,
            pl.BlockSpec((4 * nh, tm), lambda i, c: (0, i)),
            pl.BlockSpec((ppc, PEER_PIECE, d), lambda i, c: (c, 0, 0)),
            pl.BlockSpec((ppc, d, PEER_PIECE), lambda i, c: (c, 0, 0)),
            pl.BlockSpec((tm, d), lambda i, c: (i, 0)),
            pl.BlockSpec((1, 1, d), lambda i, c: (seg_fn(i), 0, 0)),
            pl.BlockSpec((1, d), lambda i, c: (0, 0)),
        ],
        out_specs=pl.BlockSpec((tm, d), lambda i, c: (i, 0)),
        out_shape=jax.ShapeDtypeStruct((t, d), F32),
        scratch_shapes=[
            pltpu.VMEM((d, tm), F32),
            pltpu.VMEM((nh, N_KEYS, tm), F32),
            pltpu.VMEM((PEER_PIECE, tm), F32),
            pltpu.VMEM((PEER_PIECE, tm), F32),
            pltpu.VMEM((PEER_PIECE, tm), BF16),
            pltpu.VMEM((PEER_PIECE, tm), BF16),
        ],
        compiler_params=_cparams("arbitrary", "arbitrary"),
        name="peer_experts",
    )(h2, s1, s2, rout, u_bf, vt_bf, x, gate, final_g)


def _peer(x, seg_fn, span, norm_g, shift, scale, gate, wq_t, keys, u_bf, vt_bf, final_g, final_norm):
    tm_r = min(512, span)
    tm_e = min(512, span)
    h2, s1, s2, rout = _peer_route(x, lambda i: seg_fn(i, tm_r), tm_r, norm_g, shift, scale, wq_t, keys)
    return _peer_experts(h2, s1, s2, rout, u_bf, vt_bf, x, lambda i: seg_fn(i, tm_e), tm_e, gate,
                         final_g, final_norm)


def _rope_tables(n_lat):
    rows = n_lat // GRID_W
    row = jnp.broadcast_to(jnp.arange(rows)[:, None], (rows, GRID_W)).reshape(-1).astype(F32)
    col = jnp.broadcast_to(jnp.arange(GRID_W)[None, :], (rows, GRID_W)).reshape(-1).astype(F32)
    inv = ROPE_BASE ** (-jnp.arange(0, ROPE_AXIS_DIM, 2, dtype=F32) / ROPE_AXIS_DIM)
    ang_r = row[:, None] * inv
    ang_c = col[:, None] * inv
    cr, sr, cc, sc = jnp.cos(ang_r), jnp.sin(ang_r), jnp.cos(ang_c), jnp.sin(ang_c)
    cos64 = jnp.concatenate([cr, cr, cc, cc], axis=-1)
    sin64 = jnp.concatenate([-sr, sr, -sc, sc], axis=-1)
    reps = LANES // SWA_HEAD_DIM
    return jnp.tile(cos64, (1, reps)), jnp.tile(sin64, (1, reps))


def _swa_weight_layout(w_in):
    qw = SWA_Q_HEADS * SWA_HEAD_DIM
    kvw = SWA_KV_HEADS * SWA_HEAD_DIM
    half = ROPE_AXIS_DIM // 2
    q_cols = np.arange(qw)
    q_swap = (q_cols // SWA_HEAD_DIM) * SWA_HEAD_DIM + ((q_cols % SWA_HEAD_DIM) ^ half)
    dup = np.arange(SWA_KV_HEADS * 2 * SWA_HEAD_DIM)
    kv_dup = (dup // (2 * SWA_HEAD_DIM)) * SWA_HEAD_DIM + (dup % SWA_HEAD_DIM)
    kv_dup_swap = (kv_dup // SWA_HEAD_DIM) * SWA_HEAD_DIM + ((kv_dup % SWA_HEAD_DIM) ^ half)
    wq = w_in[:, :qw]
    wk = w_in[:, qw:qw + kvw]
    wv = w_in[:, qw + kvw:]
    w_cat = jnp.concatenate([wq, wq[:, q_swap], wk[:, kv_dup], wk[:, kv_dup_swap], wv[:, kv_dup]], axis=1)
    w_kv = jnp.concatenate([wk[:, kv_dup], wv[:, kv_dup]], axis=1)
    return w_cat.astype(BF16), w_kv.astype(BF16)


def kernel(x, c, ctx, c_ctx, ada_w, ada_b, norm1_g, norm2_g, gdn_w_in, gdn_conv_w, gdn_a_log, gdn_dt_bias,
           gdn_norm_g, gdn_w_out, swa_w_in, swa_sinks, swa_w_out, peer_w_query, peer_sub_keys, peer_u, peer_v,
           final_g):
    return _forward(x, c, ctx, c_ctx, ada_w, ada_b, norm1_g, norm2_g, gdn_w_in, gdn_conv_w, gdn_a_log,
                    gdn_dt_bias, gdn_norm_g, gdn_w_out, swa_w_in, swa_sinks, swa_w_out, peer_w_query,
                    peer_sub_keys, peer_u, peer_v, final_g)[0]


def _forward(x, c, ctx, c_ctx, ada_w, ada_b, norm1_g, norm2_g, gdn_w_in, gdn_conv_w, gdn_a_log, gdn_dt_bias,
             gdn_norm_g, gdn_w_out, swa_w_in, swa_sinks, swa_w_out, peer_w_query, peer_sub_keys, peer_u, peer_v,
             final_g):
    b, l, d = x.shape
    nctx = ctx.shape[1]
    depth = ada_w.shape[0]
    assert depth == 2 and l % 256 == 0 and nctx % 128 == 0 and b + 1 <= 16

    mod_rows = 16
    cmat = jnp.zeros((mod_rows, d), F32).at[:b].set(c).at[b].set(c_ctx)
    ada = _ada_params(cmat, ada_w, ada_b)
    mods = ada.reshape(depth, mod_rows, ADA_CHUNKS, d).transpose(0, 2, 1, 3)[:, :, :, None, :]

    xl = x.reshape(b * l, d)
    xc = ctx.reshape(b * nctx, d)

    def lat_seg(i, tm):
        return i // (l // tm)

    def ctx_seg(i, tm):
        return b

    streams = ((lat_seg, l), (ctx_seg, nctx))

    def peer_weights(li):
        wq_t = peer_w_query[li].T.astype(BF16)
        keys = peer_sub_keys[li].reshape(PEER_HEADS * 2, N_KEYS, -1).astype(BF16)
        u_p = peer_u[li].astype(BF16).reshape(-1, PEER_PIECE, d)
        vt_p = peer_v[li].astype(BF16).reshape(-1, PEER_PIECE, d).transpose(0, 2, 1)
        return wq_t, keys, u_p, vt_p

    final_row = final_g.reshape(1, d)

    sh1, sc1, g1, sh2, sc2, g2 = (mods[0, j] for j in range(ADA_CHUNKS))
    n1 = norm1_g[0].reshape(1, d)
    n2 = norm2_g[0].reshape(1, d)
    width = GDN_HEADS * GDN_DK
    w_in = gdn_w_in[0]
    w_main = w_in[:, :4 * width].astype(BF16)
    w_abt = w_in[:, 4 * width:].T
    conv_w8 = jnp.zeros((8, 3 * width), F32).at[:GDN_CONV].set(gdn_conv_w[0])
    alog_rows = jnp.broadcast_to(gdn_a_log[0].reshape(2 * GDN_HEADS, 1), (2 * GDN_HEADS, LANES))
    dt_rows = jnp.broadcast_to(gdn_dt_bias[0].reshape(2 * GDN_HEADS, 1), (2 * GDN_HEADS, LANES))
    gdn_ng = gdn_norm_g[0].reshape(1, GDN_DK)
    w_out0 = gdn_w_out[0].astype(BF16)

    tok = {}
    for name, xs, (seg, slen) in (("ctx", xc, streams[1]), ("lat", xl, streams[0])):
        tm = min(512, slen)
        p_main, ab_t = _gdn_inproj(xs, lambda i: seg(i, tm), tm, n1, sh1, sc1, w_main, w_abt)
        qh, kh, vh = _gdn_prep(p_main, b, slen, conv_w8)
        tok[name] = (xs, seg, slen, tm, p_main, ab_t, qh, kh, vh)

    s0 = jnp.zeros((b, 2, GDN_HEADS, GDN_DK, GDN_DK), F32)
    new_x = {}
    for name in ("ctx", "lat"):
        xs, seg, slen, tm, p_main, ab_t, qh, kh, vh = tok[name]
        o_f, o_b, s0 = _gdn_scan(qh, kh, vh, ab_t, b, slen, alog_rows, dt_rows, s0)
        new_x[name] = _gdn_out(o_f, o_b, p_main, xs, lambda i: seg(i, tm), tm, g1, gdn_ng, w_out0)

    wq_t, keys, u_bf, vt_bf = peer_weights(0)
    xl = _peer(new_x["lat"], lat_seg, l, n2, sh2, sc2, g2, wq_t, keys, u_bf, vt_bf, final_row, False)
    xc = _peer(new_x["ctx"], ctx_seg, b * nctx, n2, sh2, sc2, g2, wq_t, keys, u_bf, vt_bf, final_row, False)
    streams_seen = dict(x1=new_x["lat"], xc1=new_x["ctx"], x2=xl, xc2=xc)

    sh1, sc1, g1, sh2, sc2, g2 = (mods[1, j] for j in range(ADA_CHUNKS))
    n1 = norm1_g[1].reshape(1, d)
    n2 = norm2_g[1].reshape(1, d)
    w_cat, w_kv = _swa_weight_layout(swa_w_in[0])
    cos_t, sin_t = _rope_tables(l)
    tm = min(512, l)
    q, k2, v2 = _swa_inproj(xl, lambda i: lat_seg(i, tm), tm, n1, sh1, sc1, w_cat, cos_t, sin_t)
    tmc = min(512, nctx)
    k2c, v2c = _swa_ctx_kv(xc, lambda i: ctx_seg(i, tmc), tmc, n1, sh1, sc1, w_kv)
    sink_rows = jnp.broadcast_to(swa_sinks[0].reshape(SWA_Q_HEADS, 1), (SWA_Q_HEADS, LANES))
    att = _attention(q, k2, v2, k2c, v2c, sink_rows, b, l, nctx)
    xl = _proj_residual(att, xl, lambda i: lat_seg(i, tm), tm, g1, swa_w_out[0].astype(BF16))
    streams_seen["x3"] = xl
    wq_t, keys, u_bf, vt_bf = peer_weights(1)
    out = _peer(xl, lat_seg, l, n2, sh2, sc2, g2, wq_t, keys, u_bf, vt_bf, final_row, True)
    return out.reshape(b, l, d), streams_seen
```

```python
import functools
import math

import numpy as np
import jax
import jax.numpy as jnp
from jax import lax
from jax.experimental import pallas as pl
from jax.experimental.pallas import tpu as pltpu

F32 = jnp.float32
BF16 = jnp.bfloat16

GRID_W = 64
GDN_HEADS = 8
GDN_DK = 128
GDN_CONV = 5
SWA_Q_HEADS = 16
SWA_KV_HEADS = 4
SWA_HEAD_DIM = 64
SWA_GROUP = SWA_Q_HEADS // SWA_KV_HEADS
ROPE_BASE = 10000.0
ROPE_AXIS_DIM = SWA_HEAD_DIM // 2
PEER_HEADS = 8
N_KEYS = 128
PEER_TOPK = 16
ADA_CHUNKS = 6
RMS_EPS = 1e-6
NEG_INF = -1e30

LANES = 128
SUBLANES_BF16 = 16
VMEM_LIMIT_BYTES = 60 * 1024 * 1024

CHUNK = LANES
ATTN_BLOCK = 128
ATTN_GROUPS_PER_BATCH = 2
EXPERT_CHUNK = 4096
PEER_PIECE = 256
ADA_TN = 1024
LOWEST = -3.0e38
INV_SQRT2 = 0.7071067811865476


def _cparams(*sem):
    return pltpu.CompilerParams(dimension_semantics=sem, vmem_limit_bytes=VMEM_LIMIT_BYTES)


def _dot(a, b):
    return jnp.dot(a, b, preferred_element_type=F32)


def _dot_nt(a, b):
    return lax.dot_general(a, b, (((1,), (1,)), ((), ())), preferred_element_type=F32)


def _dot_tn(a, b):
    return lax.dot_general(a, b, (((0,), (0,)), ((), ())), preferred_element_type=F32)


def _split2(a):
    hi = a.astype(BF16)
    lo = (a - hi.astype(F32)).astype(BF16)
    return hi, lo


def _split3(a):
    a1 = a.astype(BF16)
    r1 = a - a1.astype(F32)
    a2 = r1.astype(BF16)
    a3 = (r1 - a2.astype(F32)).astype(BF16)
    return a1, a2, a3


def _sigmoid(x):
    return 1.0 / (1.0 + jnp.exp(-x))


def _softplus(x):
    return jnp.maximum(x, 0.0) + jnp.log1p(jnp.exp(-jnp.abs(x)))


def _modulated(x, g, shift, scale):
    ms = jnp.mean(x * x, axis=-1, keepdims=True)
    y = x * lax.rsqrt(ms + RMS_EPS)
    return (y * g) * (1.0 + scale) + shift


def _ada_kernel(c_ref, w_ref, b_ref, o_ref):
    c = c_ref[...]
    s = c * _sigmoid(c)
    sh, sl = _split2(s)
    wh, wl = _split2(w_ref[0])
    o_ref[0] = _dot(sh, wh) + _dot(sh, wl) + _dot(sl, wh) + b_ref[0]


def _ada_params(cmat, ada_w, ada_b):
    depth, d, n = ada_w.shape
    rows = cmat.shape[0]
    return pl.pallas_call(
        _ada_kernel,
        grid=(depth, n // ADA_TN),
        in_specs=[
            pl.BlockSpec((rows, d), lambda l, j: (0, 0)),
            pl.BlockSpec((1, d, ADA_TN), lambda l, j: (l, 0, j)),
            pl.BlockSpec((1, 1, ADA_TN), lambda l, j: (l, 0, j)),
        ],
        out_specs=pl.BlockSpec((1, rows, ADA_TN), lambda l, j: (l, 0, j)),
        out_shape=jax.ShapeDtypeStruct((depth, rows, n), F32),
        compiler_params=_cparams("arbitrary", "arbitrary"),
        name="ada_params",
    )(cmat, ada_w, ada_b.reshape(depth, 1, n))


def _gdn_inproj_kernel(x_ref, g_ref, sh_ref, sc_ref, w_ref, wab_ref, p_ref, ab_ref):
    h = _modulated(x_ref[...], g_ref[...], sh_ref[0], sc_ref[0])
    hb = h.astype(BF16)
    p_ref[...] = _dot(hb, w_ref[...]).astype(BF16)
    hl = (h - hb.astype(F32)).astype(BF16)
    wh, wl = _split2(wab_ref[...])
    ab_ref[...] = _dot_nt(wh, hb) + _dot_nt(wh, hl) + _dot_nt(wl, hb)


def _gdn_inproj(x, seg_fn, tm, norm_g, shift, scale, w_main, w_abt):
    t, d = x.shape
    n = w_main.shape[1]
    na = w_abt.shape[0]
    return pl.pallas_call(
        _gdn_inproj_kernel,
        grid=(t // tm,),
        in_specs=[
            pl.BlockSpec((tm, d), lambda i: (i, 0)),
            pl.BlockSpec((1, d), lambda i: (0, 0)),
            pl.BlockSpec((1, 1, d), lambda i: (seg_fn(i), 0, 0)),
            pl.BlockSpec((1, 1, d), lambda i: (seg_fn(i), 0, 0)),
            pl.BlockSpec((d, n), lambda i: (0, 0)),
            pl.BlockSpec((na, d), lambda i: (0, 0)),
        ],
        out_specs=[
            pl.BlockSpec((tm, n), lambda i: (i, 0)),
            pl.BlockSpec((na, tm), lambda i: (0, i)),
        ],
        out_shape=[jax.ShapeDtypeStruct((t, n), BF16), jax.ShapeDtypeStruct((na, t), F32)],
        compiler_params=_cparams("arbitrary"),
        name="gdn_inproj",
    )(x, norm_g, shift, scale, w_main, w_abt)


CONV_HALO = SUBLANES_BF16
CONV_PAD = GDN_CONV // 2


def _gdn_prep_kernel(main_ref, prev_ref, next_ref, w_ref, q_ref, k_ref, v_ref, ext_scr):
    t = pl.program_id(1)
    nt = pl.num_programs(1)
    tl = main_ref.shape[0]
    base = CONV_HALO
    ext_scr[base:base + tl, :] = main_ref[...].astype(F32)
    prev = prev_ref[...].astype(F32)[CONV_HALO - CONV_PAD:, :]
    nxt = next_ref[...].astype(F32)[:CONV_PAD, :]
    ext_scr[base - CONV_PAD:base, :] = prev * (t > 0).astype(F32)
    ext_scr[base + tl:base + tl + CONV_PAD, :] = nxt * (t < nt - 1).astype(F32)
    nblk = main_ref.shape[1] // LANES
    for cb in range(nblk):
        cols = slice(cb * LANES, (cb + 1) * LANES)
        acc = jnp.zeros((tl, LANES), F32)
        for j in range(GDN_CONV):
            acc = acc + w_ref[j:j + 1, cols] * ext_scr[base - CONV_PAD + j:base - CONV_PAD + j + tl, cols]
        y = acc * _sigmoid(acc)
        which, head = divmod(cb, GDN_HEADS)
        if which < 2:
            y = y * lax.rsqrt(jnp.sum(y * y, axis=-1, keepdims=True) + 1e-6)
            if which == 0:
                y = y * (GDN_DK ** -0.5)
        (q_ref, k_ref, v_ref)[which][head] = y.astype(BF16)


def _gdn_prep(p_main, nseq, seqlen, conv_w8):
    t = p_main.shape[0]
    tl = min(256, seqlen)
    tps = seqlen // tl
    ncol = 3 * GDN_HEADS * GDN_DK
    hb = tl // CONV_HALO
    last_halo = t // CONV_HALO - 1
    out = jax.ShapeDtypeStruct((GDN_HEADS, t, GDN_DK), BF16)
    ospec = pl.BlockSpec((GDN_HEADS, tl, GDN_DK), lambda s, i: (0, s * tps + i, 0))
    return pl.pallas_call(
        _gdn_prep_kernel,
        grid=(nseq, tps),
        in_specs=[
            pl.BlockSpec((tl, ncol), lambda s, i: (s * tps + i, 0)),
            pl.BlockSpec((CONV_HALO, ncol), lambda s, i: (jnp.maximum((s * tps + i) * hb - 1, 0), 0)),
            pl.BlockSpec((CONV_HALO, ncol), lambda s, i: (jnp.minimum((s * tps + i + 1) * hb, last_halo), 0)),
            pl.BlockSpec((8, ncol), lambda s, i: (0, 0)),
        ],
        out_specs=[ospec, ospec, ospec],
        out_shape=[out, out, out],
        scratch_shapes=[pltpu.VMEM((tl + 2 * CONV_HALO, ncol), F32)],
        compiler_params=_cparams("arbitrary", "arbitrary"),
        name="gdn_prep",
    )(p_main, p_main, p_main, conv_w8)


INV_BASE = 16


def _each(fn, *lists):
    return [fn(*args) for args in zip(*lists)]


def _unit_triangular_inverses(lmats, ri, ci):
    c = lmats[0].shape[0]
    eye = jnp.where(ri == ci, 1.0, 0.0)
    same_base = (ri // INV_BASE) == (ci // INV_BASE)
    a = [jnp.where(same_base, -l, 0.0) for l in lmats]
    t = [eye + x for x in a]
    ap = [x.astype(BF16) for x in a]
    for _ in range(1, int(math.log2(INV_BASE))):
        ap = [_dot(x, x).astype(BF16) for x in ap]
        t = _each(lambda tt, x: tt + _dot(tt.astype(BF16), x), t, ap)
    size = INV_BASE
    while size < c:
        same_small = (ri // size) == (ci // size)
        same_big = (ri // (2 * size)) == (ci // (2 * size))
        couple = jnp.logical_and(same_big, jnp.logical_not(same_small))
        l1 = [jnp.where(couple, l, 0.0).astype(BF16) for l in lmats]
        tb = [x.astype(BF16) for x in t]
        mid = _each(lambda l, x: _dot(l, x).astype(BF16), l1, tb)
        t = _each(lambda tt, x, m: tt - _dot(x, m), t, tb, mid)
        size *= 2
    return t


def _delta_chunks(q, k, v, gc_row, beta_row, tot_row, s_prev, lower):
    c = q[0].shape[0]
    dk = k[0].shape[1]
    ri = lax.broadcasted_iota(jnp.int32, (c, c), 0)
    ci = lax.broadcasted_iota(jnp.int32, (c, c), 1)
    incl = [ri >= ci if lo else ri <= ci for lo in lower]
    strict = [ri > ci if lo else ri < ci for lo in lower]
    gc_rb = [jnp.broadcast_to(g, (c, c)) for g in gc_row]
    gc_cb = [g.T for g in gc_rb]
    beta_cb = [jnp.broadcast_to(b, (c, c)).T for b in beta_row]
    tot_b = [jnp.broadcast_to(t, (c, c)) for t in tot_row]
    decay = _each(lambda m, gc, gr: jnp.where(m, jnp.exp(jnp.where(m, gc - gr, 0.0)), 0.0), incl, gc_cb, gc_rb)
    kf = [x.astype(F32) for x in k]
    kk = _each(_dot_nt, k, k)
    qk = _each(_dot_nt, q, k)
    lmat = _each(lambda x, b, m, d: (x * b) * jnp.where(m, d, 0.0), kk, beta_cb, strict, decay)
    aqk = _each(lambda x, d: (x * d).astype(BF16), qk, decay)
    eg_cb = [jnp.exp(g) for g in gc_cb]
    rhs = _each(lambda vv, kx, b, e: jnp.concatenate([vv.astype(F32) * b, kx * (b * e)], axis=1).astype(BF16),
                v, kf, beta_cb, eg_cb)
    t_inv = _unit_triangular_inverses(lmat, ri, ci)
    x = _each(lambda t, r: _dot(t.astype(BF16), r), t_inv, rhs)
    sb = [s.astype(BF16) for s in s_prev]
    v_new = _each(lambda xx, s: xx[:, :dk] - _dot(xx[:, dk:].astype(BF16), s), x, sb)
    vb = [vn.astype(BF16) for vn in v_new]
    qd = _each(lambda qq, e: (qq.astype(F32) * e).astype(BF16), q, eg_cb)
    o = _each(lambda a, s, w, vv: _dot(a, s) + _dot(w, vv), qd, sb, aqk, vb)
    kd = _each(lambda kx, t, g: (kx * jnp.exp(t - g)).astype(BF16), kf, tot_b, gc_cb)
    s_next = _each(lambda s, t, a, vv: s * jnp.exp(t) + _dot_tn(a, vv), s_prev, tot_b, kd, vb)
    return o, s_next


def _gdn_scan_kernel(qf_ref, kf_ref, vf_ref, qb_ref, kb_ref, vb_ref, abf_ref, abb_ref, alog_ref, dt_ref,
                     s0_ref, of_ref, ob_ref, sfin_ref, s_scr):
    i = pl.program_id(1)
    n = pl.num_programs(1)
    nh = GDN_HEADS

    @pl.when(i == 0)
    def _():
        s_scr[...] = s0_ref[0]

    c = CHUNK
    ri = lax.broadcasted_iota(jnp.int32, (c, c), 0)
    ci = lax.broadcasted_iota(jnp.int32, (c, c), 1)
    one = jnp.ones((c, c), BF16)
    upper_incl = jnp.where(ri <= ci, 1.0, 0.0).astype(BF16)
    lower_incl = jnp.where(ri >= ci, 1.0, 0.0).astype(BF16)

    def cums(g, tri):
        g1, g2, g3 = _split3(g)
        return (_dot(g1, tri) + _dot(g2, tri) + _dot(g3, tri),
                _dot(g1, one) + _dot(g2, one) + _dot(g3, one))

    g_f = -jnp.exp(alog_ref[0:nh, :]) * _softplus(abf_ref[0:nh, :] + dt_ref[0:nh, :])
    g_b = -jnp.exp(alog_ref[nh:2 * nh, :]) * _softplus(abb_ref[nh:2 * nh, :] + dt_ref[nh:2 * nh, :])
    gc_f, tot_f = cums(g_f, upper_incl)
    gc_b, tot_b = cums(g_b, lower_incl)
    rows = jnp.concatenate([gc_f, gc_b, _sigmoid(abf_ref[2 * nh:3 * nh, :]), _sigmoid(abb_ref[3 * nh:4 * nh, :]),
                            tot_f, tot_b], axis=0)

    row = lambda r: rows[r:r + 1]
    heads = range(nh)
    o, s_next = _delta_chunks(
        q=[qf_ref[h] for h in heads] + [qb_ref[h] for h in heads],
        k=[kf_ref[h] for h in heads] + [kb_ref[h] for h in heads],
        v=[vf_ref[h] for h in heads] + [vb_ref[h] for h in heads],
        gc_row=[row(h) for h in heads] + [row(nh + h) for h in heads],
        beta_row=[row(2 * nh + h) for h in heads] + [row(3 * nh + h) for h in heads],
        tot_row=[row(4 * nh + h) for h in heads] + [row(5 * nh + h) for h in heads],
        s_prev=[s_scr[0, h] for h in heads] + [s_scr[1, h] for h in heads],
        lower=[True] * nh + [False] * nh)
    for h in heads:
        s_scr[0, h] = s_next[h]
        s_scr[1, h] = s_next[nh + h]
        of_ref[h] = o[h].astype(BF16)
        ob_ref[h] = o[nh + h].astype(BF16)

    @pl.when(i == n - 1)
    def _():
        sfin_ref[0] = s_scr[...]


def _gdn_scan(qh, kh, vh, ab_t, nseq, seqlen, alog_rows, dt_rows, s0):
    t = qh.shape[1]
    nt = seqlen // CHUNK
    nab = ab_t.shape[0]
    hspec_f = pl.BlockSpec((GDN_HEADS, CHUNK, GDN_DK), lambda s, i: (0, s * nt + i, 0))
    hspec_b = pl.BlockSpec((GDN_HEADS, CHUNK, GDN_DK), lambda s, i: (0, s * nt + nt - 1 - i, 0))
    sspec = pl.BlockSpec((1, 2, GDN_HEADS, GDN_DK, GDN_DK), lambda s, i: (s, 0, 0, 0, 0))
    oshape = jax.ShapeDtypeStruct((GDN_HEADS, t, GDN_DK), BF16)
    return pl.pallas_call(
        _gdn_scan_kernel,
        grid=(nseq, nt),
        in_specs=[
            hspec_f, hspec_f, hspec_f, hspec_b, hspec_b, hspec_b,
            pl.BlockSpec((nab, CHUNK), lambda s, i: (0, s * nt + i)),
            pl.BlockSpec((nab, CHUNK), lambda s, i: (0, s * nt + nt - 1 - i)),
            pl.BlockSpec((2 * GDN_HEADS, LANES), lambda s, i: (0, 0)),
            pl.BlockSpec((2 * GDN_HEADS, LANES), lambda s, i: (0, 0)),
            sspec,
        ],
        out_specs=[hspec_f, hspec_b, sspec],
        out_shape=[oshape, oshape, jax.ShapeDtypeStruct(s0.shape, F32)],
        scratch_shapes=[
            pltpu.VMEM((2, GDN_HEADS, GDN_DK, GDN_DK), F32),
        ],
        compiler_params=_cparams("arbitrary", "arbitrary"),
        name="gdn_scan",
    )(qh, kh, vh, qh, kh, vh, ab_t, ab_t, alog_rows, dt_rows, s0)


def _gdn_out_kernel(of_ref, ob_ref, z_ref, x_ref, gate_ref, ng_ref, w_ref, o_ref):
    parts = []
    for h in range(GDN_HEADS):
        o = of_ref[h].astype(F32) + ob_ref[h].astype(F32)
        on = o * lax.rsqrt(jnp.mean(o * o, axis=-1, keepdims=True) + RMS_EPS) * ng_ref[...]
        z = z_ref[:, h * GDN_DK:(h + 1) * GDN_DK].astype(F32)
        parts.append((on * (z * _sigmoid(z))).astype(BF16))
    a = jnp.concatenate(parts, axis=-1)
    o_ref[...] = x_ref[...] + gate_ref[0] * _dot(a, w_ref[...])


def _gdn_out(o_f, o_b, p_main, x, seg_fn, tm, gate, norm_g, w_out):
    t, d = x.shape
    width = GDN_HEADS * GDN_DK
    zblk = 3
    hspec = pl.BlockSpec((GDN_HEADS, tm, GDN_DK), lambda i: (0, i, 0))
    return pl.pallas_call(
        _gdn_out_kernel,
        grid=(t // tm,),
        in_specs=[
            hspec, hspec,
            pl.BlockSpec((tm, width), lambda i: (i, zblk)),
            pl.BlockSpec((tm, d), lambda i: (i, 0)),
            pl.BlockSpec((1, 1, d), lambda i: (seg_fn(i), 0, 0)),
            pl.BlockSpec((1, GDN_DK), lambda i: (0, 0)),
            pl.BlockSpec((width, d), lambda i: (0, 0)),
        ],
        out_specs=pl.BlockSpec((tm, d), lambda i: (i, 0)),
        out_shape=jax.ShapeDtypeStruct((t, d), F32),
        compiler_params=_cparams("arbitrary"),
        name="gdn_out",
    )(o_f, o_b, p_main, x, gate, norm_g, w_out)


def _swa_inproj_kernel(x_ref, g_ref, sh_ref, sc_ref, w_ref, cos_ref, sin_ref, q_ref, k_ref, v_ref):
    h = _modulated(x_ref[...], g_ref[...], sh_ref[0], sc_ref[0]).astype(BF16)
    p = _dot(h, w_ref[...])
    qw = SWA_Q_HEADS * SWA_HEAD_DIM
    kw = SWA_KV_HEADS * 2 * SWA_HEAD_DIM
    cos = cos_ref[...]
    sin = sin_ref[...]
    cq = jnp.concatenate([cos] * (qw // LANES), axis=-1)
    sq = jnp.concatenate([sin] * (qw // LANES), axis=-1)
    ck = jnp.concatenate([cos] * (kw // LANES), axis=-1)
    sk = jnp.concatenate([sin] * (kw // LANES), axis=-1)
    half = ROPE_AXIS_DIM // 2

    def swap_halves(y):
        n = y.shape[1]
        from_below = pltpu.roll(y, half, 1)
        from_above = pltpu.roll(y, n - half, 1)
        upper = (lax.broadcasted_iota(jnp.int32, (1, n), 1) & half) != 0
        return jnp.where(upper, from_below, from_above)

    pq = p[:, :qw]
    pk = p[:, qw:qw + kw]
    q = pq * cq + swap_halves(pq) * sq
    k = pk * ck + swap_halves(pk) * sk
    q_ref[...] = (q * (SWA_HEAD_DIM ** -0.5)).astype(BF16)
    k_ref[...] = k.astype(BF16)
    v_ref[...] = p[:, qw + kw:].astype(BF16)


def _swa_inproj(x, seg_fn, tm, norm_g, shift, scale, w_cat, cos_t, sin_t):
    t, d = x.shape
    n = w_cat.shape[1]
    qw = SWA_Q_HEADS * SWA_HEAD_DIM
    kw = SWA_KV_HEADS * 2 * SWA_HEAD_DIM
    tps = cos_t.shape[0] // tm
    return pl.pallas_call(
        _swa_inproj_kernel,
        grid=(t // tm,),
        in_specs=[
            pl.BlockSpec((tm, d), lambda i: (i, 0)),
            pl.BlockSpec((1, d), lambda i: (0, 0)),
            pl.BlockSpec((1, 1, d), lambda i: (seg_fn(i), 0, 0)),
            pl.BlockSpec((1, 1, d), lambda i: (seg_fn(i), 0, 0)),
            pl.BlockSpec((d, n), lambda i: (0, 0)),
            pl.BlockSpec((tm, LANES), lambda i: (i % tps, 0)),
            pl.BlockSpec((tm, LANES), lambda i: (i % tps, 0)),
        ],
        out_specs=[
            pl.BlockSpec((tm, qw), lambda i: (i, 0)),
            pl.BlockSpec((tm, kw), lambda i: (i, 0)),
            pl.BlockSpec((tm, kw), lambda i: (i, 0)),
        ],
        out_shape=[jax.ShapeDtypeStruct((t, qw), BF16), jax.ShapeDtypeStruct((t, kw), BF16),
                   jax.ShapeDtypeStruct((t, kw), BF16)],
        compiler_params=_cparams("arbitrary"),
        name="swa_inproj",
    )(x, norm_g, shift, scale, w_cat, cos_t, sin_t)


def _swa_ctx_kv_kernel(x_ref, g_ref, sh_ref, sc_ref, w_ref, k_ref, v_ref):
    h = _modulated(x_ref[...], g_ref[...], sh_ref[0], sc_ref[0]).astype(BF16)
    p = _dot(h, w_ref[...])
    kw = k_ref.shape[1]
    k_ref[...] = p[:, :kw].astype(BF16)
    v_ref[...] = p[:, kw:].astype(BF16)


def _swa_ctx_kv(xc, seg_fn, tm, norm_g, shift, scale, w_kv):
    t, d = xc.shape
    kw = SWA_KV_HEADS * 2 * SWA_HEAD_DIM
    return pl.pallas_call(
        _swa_ctx_kv_kernel,
        grid=(t // tm,),
        in_specs=[
            pl.BlockSpec((tm, d), lambda i: (i, 0)),
            pl.BlockSpec((1, d), lambda i: (0, 0)),
            pl.BlockSpec((1, 1, d), lambda i: (seg_fn(i), 0, 0)),
            pl.BlockSpec((1, 1, d), lambda i: (seg_fn(i), 0, 0)),
            pl.BlockSpec((d, 2 * kw), lambda i: (0, 0)),
        ],
        out_specs=[pl.BlockSpec((tm, kw), lambda i: (i, 0)), pl.BlockSpec((tm, kw), lambda i: (i, 0))],
        out_shape=[jax.ShapeDtypeStruct((t, kw), BF16), jax.ShapeDtypeStruct((t, kw), BF16)],
        compiler_params=_cparams("arbitrary"),
        name="swa_ctx_kv",
    )(xc, norm_g, shift, scale, w_kv)


def _attn_kernel(q_ref, kp_ref, kc_ref, kn_ref, vp_ref, vc_ref, vn_ref, kx_ref, vx_ref, sink_ref, o_ref):
    n = pl.program_id(1)
    nb = pl.num_programs(1)
    blk = ATTN_BLOCK
    nctx = kx_ref.shape[0]
    ri = lax.broadcasted_iota(jnp.int32, (blk, blk), 0)
    ci = lax.broadcasted_iota(jnp.int32, (blk, blk), 1)
    never = 2 * blk
    ok_prev = (ci - ri) >= jnp.where(n > 0, 0, never)
    ok_next = (ri - ci) >= jnp.where(n < nb - 1, 0, never)
    lane = lax.broadcasted_iota(jnp.int32, (1, LANES), 1)
    lo_half = lane < SWA_HEAD_DIM
    keep_lo = jnp.where(lo_half, 1.0, 0.0).astype(BF16)
    keep_hi = jnp.where(lo_half, 0.0, 1.0).astype(BF16)
    gw = 2 * SWA_HEAD_DIM
    for g0 in range(0, SWA_KV_HEADS, ATTN_GROUPS_PER_BATCH):
        qps, kms, vcats, sinks, cols = [], [], [], [], []
        for g in range(g0, g0 + ATTN_GROUPS_PER_BATCH):
            gs = slice(g * gw, (g + 1) * gw)
            kcat = jnp.concatenate([kp_ref[:, gs], kc_ref[:, gs], kn_ref[:, gs], kx_ref[:, gs]], axis=0)
            vcat = jnp.concatenate([vp_ref[:, gs], vc_ref[:, gs], vn_ref[:, gs], vx_ref[:, gs]], axis=0)
            k_halves = (kcat * keep_lo, kcat * keep_hi)
            for pr in range(SWA_GROUP // 2):
                col = (g * (SWA_GROUP // 2) + pr) * LANES
                qp = q_ref[:, col:col + LANES]
                for half in range(2):
                    head = g * SWA_GROUP + pr * 2 + half
                    qps.append(qp)
                    kms.append(k_halves[half])
                    vcats.append(vcat)
                    sinks.append(jnp.broadcast_to(sink_ref[head:head + 1, :], (blk, LANES))[:, 0:1])
                    cols.append(col)
        s = _each(_dot_nt, qps, kms)
        s = [jnp.concatenate([jnp.where(ok_prev, x[:, :blk], NEG_INF), x[:, blk:2 * blk],
                              jnp.where(ok_next, x[:, 2 * blk:3 * blk], NEG_INF), x[:, 3 * blk:]], axis=1) for x in s]
        m = _each(lambda x, sk: jnp.maximum(jnp.max(x, axis=-1, keepdims=True), sk), s, sinks)
        p = _each(lambda x, mm: jnp.exp(x - mm), s, m)
        den = _each(lambda x, sk, mm: jnp.sum(x, axis=-1, keepdims=True) + jnp.exp(sk - mm), p, sinks, m)
        o = _each(lambda x, vv, dd: _dot(x.astype(BF16), vv) / dd, p, vcats, den)
        for j in range(0, len(o), 2):
            o_ref[:, cols[j]:cols[j] + LANES] = jnp.where(lo_half, o[j], o[j + 1]).astype(BF16)


def _attention(q, k2, v2, k2c, v2c, sink_rows, nseq, seqlen, nctx):
    t, qw = q.shape
    kw = k2.shape[1]
    blk = ATTN_BLOCK
    nb = seqlen // blk
    kv_prev = pl.BlockSpec((blk, kw), lambda s, i: (s * nb + jnp.maximum(i - 1, 0), 0))
    kv_cur = pl.BlockSpec((blk, kw), lambda s, i: (s * nb + i, 0))
    kv_next = pl.BlockSpec((blk, kw), lambda s, i: (s * nb + jnp.minimum(i + 1, nb - 1), 0))
    kv_ctx = pl.BlockSpec((nctx, kw), lambda s, i: (s, 0))
    return pl.pallas_call(
        _attn_kernel,
        grid=(nseq, nb),
        in_specs=[
            pl.BlockSpec((blk, qw), lambda s, i: (s * nb + i, 0)),
            kv_prev, kv_cur, kv_next, kv_prev, kv_cur, kv_next, kv_ctx, kv_ctx,
            pl.BlockSpec(sink_rows.shape, lambda s, i: (0, 0)),
        ],
        out_specs=pl.BlockSpec((blk, qw), lambda s, i: (s * nb + i, 0)),
        out_shape=jax.ShapeDtypeStruct((t, qw), BF16),
        compiler_params=_cparams("arbitrary", "arbitrary"),
        name="swa_attention",
    )(q, k2, k2, k2, v2, v2, v2, k2c, v2c, sink_rows)


def _proj_residual_kernel(a_ref, x_ref, gate_ref, w_ref, o_ref):
    o_ref[...] = x_ref[...] + gate_ref[0] * _dot(a_ref[...], w_ref[...])


def _proj_residual(a, x, seg_fn, tm, gate, w):
    t, d = x.shape
    k = a.shape[1]
    return pl.pallas_call(
        _proj_residual_kernel,
        grid=(t // tm,),
        in_specs=[
            pl.BlockSpec((tm, k), lambda i: (i, 0)),
            pl.BlockSpec((tm, d), lambda i: (i, 0)),
            pl.BlockSpec((1, 1, d), lambda i: (seg_fn(i), 0, 0)),
            pl.BlockSpec((k, d), lambda i: (0, 0)),
        ],
        out_specs=pl.BlockSpec((tm, d), lambda i: (i, 0)),
        out_shape=jax.ShapeDtypeStruct((t, d), F32),
        compiler_params=_cparams("arbitrary"),
        name="proj_residual",
    )(a, x, gate, w)


N_RANKS = PEER_TOPK + 1
RANK_ROWS = 24


def _batcher_network(n):
    pairs = []

    def merge(lo, length, r):
        step = 2 * r
        if step < length:
            merge(lo, length, step)
            merge(lo + r, length, step)
            for i in range(lo + r, lo + length - r, step):
                pairs.append((i, i + r))
        else:
            pairs.append((lo, lo + r))

    def sort(lo, length):
        if length > 1:
            half = length // 2
            sort(lo, half)
            sort(lo + half, half)
            merge(lo, length, 1)

    sort(0, n)
    return pairs


_SORT_NETWORK = _batcher_network(N_KEYS // 8)


def _tree_max(vals):
    vals = list(vals)
    while len(vals) > 1:
        nxt = [jnp.maximum(vals[i], vals[i + 1]) for i in range(0, len(vals) - 1, 2)]
        if len(vals) % 2:
            nxt.append(vals[-1])
        vals = nxt
    return vals[0]


def _peer_route_kernel(x_ref, g_ref, sh_ref, sc_ref, wqt_ref, keys_ref,
                       h2_ref, s1_ref, s2_ref, rout_ref, a_scr, b_scr):
    tm = x_ref.shape[0]
    nh = PEER_HEADS
    hb = _modulated(x_ref[...], g_ref[...], sh_ref[0], sc_ref[0]).astype(BF16)
    h2_ref[...] = hb
    nlc = tm // LANES
    rows_per_head = 2 * N_KEYS

    def head_body(h):
        qt = _dot_nt(wqt_ref[h * rows_per_head:(h + 1) * rows_per_head, :], hb).astype(BF16)
        chains = []
        for p, (s_ref, top_scr) in enumerate(((s1_ref, a_scr), (s2_ref, b_scr))):
            hp = h * 2 + p
            s = _dot(keys_ref[hp], qt[p * N_KEYS:(p + 1) * N_KEYS, :])
            s_ref[h] = s
            for lc in range(nlc):
                ls = slice(lc * LANES, (lc + 1) * LANES)
                chains.append((top_scr, ls, s[:, ls]))
        lists = [[x[g * 8:(g + 1) * 8, :] for g in range(N_KEYS // 8)] for (_, _, x) in chains]
        for (i, j) in _SORT_NETWORK:
            for lst in lists:
                hi = jnp.maximum(lst[i], lst[j])
                lst[j] = jnp.minimum(lst[i], lst[j])
                lst[i] = hi
        for r in range(N_RANKS):
            depth = min(len(lists[0]), N_RANKS - r)
            for (top_scr, ls, _), lst in zip(chains, lists):
                m = jnp.max(lst[0], axis=0, keepdims=True)
                top_scr[h, r:r + 1, ls] = m
                popped = lst[0] == m
                for kq in range(depth - 1):
                    lst[kq] = jnp.where(popped, lst[kq + 1], lst[kq])
                lst[depth - 1] = jnp.where(popped, LOWEST, lst[depth - 1])

    for h in range(nh):
        head_body(h)

    for lc in range(nlc):
        ls = slice(lc * LANES, (lc + 1) * LANES)
        av = [jnp.concatenate([a_scr[h, r:r + 1, ls] for h in range(nh)], axis=0) for r in range(N_RANKS)]
        bv = [jnp.concatenate([b_scr[h, r:r + 1, ls] for h in range(nh)], axis=0) for r in range(N_RANKS)]
        groups = [[av[r] + bv[c] for c in range(N_RANKS // (r + 1))] for r in range(N_RANKS)]
        cands = [cv for grp in groups for cv in grp]
        top = groups[0][0]
        kth = []
        for it in range(N_RANKS):
            m = _tree_max([grp[0] for grp in groups])
            kth.append(m)
            for grp in groups:
                depth = min(len(grp), N_RANKS - it)
                popped = grp[0] == m
                for kq in range(depth - 1):
                    grp[kq] = jnp.where(popped, grp[kq + 1], grp[kq])
                grp[depth - 1] = jnp.where(popped, LOWEST, grp[depth - 1])
        tau = 0.5 * (kth[PEER_TOPK - 1] + kth[PEER_TOPK])
        z = jnp.zeros_like(top)
        for cv in cands:
            z = z + jnp.where(cv >= tau, jnp.exp(cv - top), 0.0)
        rout_ref[0 * nh:1 * nh, ls] = tau
        rout_ref[1 * nh:2 * nh, ls] = av[0]
        rout_ref[2 * nh:3 * nh, ls] = bv[0]
        rout_ref[3 * nh:4 * nh, ls] = 1.0 / z


def _peer_route(x, seg_fn, tm, norm_g, shift, scale, wq_t, keys):
    t, d = x.shape
    nq = wq_t.shape[0]
    nh = PEER_HEADS
    sshape = jax.ShapeDtypeStruct((nh, N_KEYS, t), F32)
    sspec = pl.BlockSpec((nh, N_KEYS, tm), lambda i: (0, 0, i))
    return pl.pallas_call(
        _peer_route_kernel,
        grid=(t // tm,),
        in_specs=[
            pl.BlockSpec((tm, d), lambda i: (i, 0)),
            pl.BlockSpec((1, d), lambda i: (0, 0)),
            pl.BlockSpec((1, 1, d), lambda i: (seg_fn(i), 0, 0)),
            pl.BlockSpec((1, 1, d), lambda i: (seg_fn(i), 0, 0)),
            pl.BlockSpec((nq, d), lambda i: (0, 0)),
            pl.BlockSpec(keys.shape, lambda i: (0, 0, 0)),
        ],
        out_specs=[
            pl.BlockSpec((tm, d), lambda i: (i, 0)),
            sspec, sspec,
            pl.BlockSpec((4 * nh, tm), lambda i: (0, i)),
        ],
        out_shape=[jax.ShapeDtypeStruct((t, d), BF16), sshape, sshape,
                   jax.ShapeDtypeStruct((4 * nh, t), F32)],
        scratch_shapes=[
            pltpu.VMEM((nh, RANK_ROWS, tm), F32),
            pltpu.VMEM((nh, RANK_ROWS, tm), F32),
        ],
        compiler_params=_cparams("arbitrary"),
        name="peer_route",
    )(x, norm_g, shift, scale, wq_t, keys)


PEER_LANE_CHUNK = 128


def _peer_expert_kernel(h2_ref, s1_ref, s2_ref, rout_ref, u_ref, vt_ref, x_ref, gate_ref, fg_ref,
                        o_ref, acc_scr, e2_scr, ht0_scr, ht1_scr, a0_scr, a1_scr, *, final_norm):
    ht_scr = (ht0_scr, ht1_scr)
    a_scr = (a0_scr, a1_scr)
    c = pl.program_id(1)
    nc = pl.num_programs(1)
    nh = PEER_HEADS
    tm = h2_ref.shape[0]
    per_step = EXPERT_CHUNK // N_KEYS

    @pl.when(c == 0)
    def _():
        acc_scr[...] = jnp.zeros_like(acc_scr)
        for h in range(nh):
            e2_scr[h] = (jnp.exp(s2_ref[h] - rout_ref[2 * nh + h:2 * nh + h + 1, :])
                         * rout_ref[3 * nh + h:3 * nh + h + 1, :])

    tau = rout_ref[0:nh, :]
    m1 = rout_ref[nh:2 * nh, :]
    m2 = rout_ref[2 * nh:3 * nh, :]
    inv_z = rout_ref[3 * nh:4 * nh, :]
    lcw = min(PEER_LANE_CHUNK, tm)
    n_pieces = u_ref.shape[0]
    keys_per_piece = PEER_PIECE // N_KEYS

    def hidden(p, slot):
        ht_scr[slot][...] = _dot_nt(u_ref[p], h2_ref[...])

    def gate_piece(p, slot):
        for kk in range(keys_per_piece):
            i = c * per_step + p * keys_per_piece + kk
            s1rows = jnp.concatenate([s1_ref[h, pl.ds(i, 1), :] for h in range(nh)], axis=0)
            thr = jnp.exp(tau - s1rows - m2) * inv_z
            coef = jnp.exp(s1rows - m1)
            erows = slice(kk * N_KEYS, (kk + 1) * N_KEYS)
            for lc in range(tm // lcw):
                ls = slice(lc * lcw, (lc + 1) * lcw)
                gate = jnp.zeros((N_KEYS, lcw), F32)
                for h in range(nh):
                    e2 = e2_scr[h, :, ls]
                    gate = gate + jnp.where(e2 >= thr[h:h + 1, ls], e2, 0.0) * coef[h:h + 1, ls]
                hh = ht_scr[slot][erows, ls]
                act = 0.5 * hh * (1.0 + lax.erf(hh * INV_SQRT2))
                a_scr[slot][erows, ls] = (act * gate).astype(BF16)

    def project(p, slot):
        acc_scr[...] += _dot(vt_ref[p], a_scr[slot][...])

    hidden(0, 0)
    hidden(1, 1)
    gate_piece(0, 0)

    def stage_pair(j, carry):
        p = 2 * j + 2
        project(p - 2, 0)
        hidden(p, 0)
        gate_piece(p - 1, 1)
        project(p - 1, 1)
        hidden(p + 1, 1)
        gate_piece(p, 0)
        return carry

    for pair in range((n_pieces - 2) // 2):
        stage_pair(pair, 0)
    gate_piece(n_pieces - 1, 1)
    project(n_pieces - 2, 0)
    project(n_pieces - 1, 1)

    @pl.when(c == nc - 1)
    def _():
        out = x_ref[...] + gate_ref[0] * acc_scr[...].T
        if final_norm:
            ms = jnp.mean(out * out, axis=-1, keepdims=True)
            out = out * lax.rsqrt(ms + RMS_EPS) * fg_ref[...]
        o_ref[...] = out


def _peer_experts(h2, s1, s2, rout, u_bf, vt_bf, x, seg_fn, tm, gate, final_g, final_norm):
    t, d = x.shape
    ppc = EXPERT_CHUNK // PEER_PIECE
    nh = PEER_HEADS
    sspec = pl.BlockSpec((nh, N_KEYS, tm), lambda i, c: (0, 0, i))
    return pl.pallas_call(
        functools.partial(_peer_expert_kernel, final_norm=final_norm),
        grid=(t // tm, u_bf.shape[0] // ppc),
        in_specs=[
            pl.BlockSpec((tm, d), lambda i, c: (i, 0)),
            sspec, sspec,
            pl.BlockSpec((4 * nh, tm), lambda i, c: (0, i)),
            pl.BlockSpec((ppc, PEER_PIECE, d), lambda i, c: (c, 0, 0)),
            pl.BlockSpec((ppc, d, PEER_PIECE), lambda i, c: (c, 0, 0)),
            pl.BlockSpec((tm, d), lambda i, c: (i, 0)),
            pl.BlockSpec((1, 1, d), lambda i, c: (seg_fn(i), 0, 0)),
            pl.BlockSpec((1, d), lambda i, c: (0, 0)),
        ],
        out_specs=pl.BlockSpec((tm, d), lambda i, c: (i, 0)),
        out_shape=jax.ShapeDtypeStruct((t, d), F32),
        scratch_shapes=[
            pltpu.VMEM((d, tm), F32),
            pltpu.VMEM((nh, N_KEYS, tm), F32),
            pltpu.VMEM((PEER_PIECE, tm), F32),
            pltpu.VMEM((PEER_PIECE, tm), F32),
            pltpu.VMEM((PEER_PIECE, tm), BF16),
            pltpu.VMEM((PEER_PIECE, tm), BF16),
        ],
        compiler_params=_cparams("arbitrary", "arbitrary"),
        name="peer_experts",
    )(h2, s1, s2, rout, u_bf, vt_bf, x, gate, final_g)


def _peer(x, seg_fn, span, norm_g, shift, scale, gate, wq_t, keys, u_bf, vt_bf, final_g, final_norm):
    tm_r = min(512, span)
    tm_e = min(512, span)
    h2, s1, s2, rout = _peer_route(x, lambda i: seg_fn(i, tm_r), tm_r, norm_g, shift, scale, wq_t, keys)
    return _peer_experts(h2, s1, s2, rout, u_bf, vt_bf, x, lambda i: seg_fn(i, tm_e), tm_e, gate,
                         final_g, final_norm)


def _rope_tables(n_lat):
    rows = n_lat // GRID_W
    row = jnp.broadcast_to(jnp.arange(rows)[:, None], (rows, GRID_W)).reshape(-1).astype(F32)
    col = jnp.broadcast_to(jnp.arange(GRID_W)[None, :], (rows, GRID_W)).reshape(-1).astype(F32)
    inv = ROPE_BASE ** (-jnp.arange(0, ROPE_AXIS_DIM, 2, dtype=F32) / ROPE_AXIS_DIM)
    ang_r = row[:, None] * inv
    ang_c = col[:, None] * inv
    cr, sr, cc, sc = jnp.cos(ang_r), jnp.sin(ang_r), jnp.cos(ang_c), jnp.sin(ang_c)
    cos64 = jnp.concatenate([cr, cr, cc, cc], axis=-1)
    sin64 = jnp.concatenate([-sr, sr, -sc, sc], axis=-1)
    reps = LANES // SWA_HEAD_DIM
    return jnp.tile(cos64, (1, reps)), jnp.tile(sin64, (1, reps))


def _swa_weight_layout(w_in):
    qw = SWA_Q_HEADS * SWA_HEAD_DIM
    kvw = SWA_KV_HEADS * SWA_HEAD_DIM
    half = ROPE_AXIS_DIM // 2
    q_cols = np.arange(qw)
    q_swap = (q_cols // SWA_HEAD_DIM) * SWA_HEAD_DIM + ((q_cols % SWA_HEAD_DIM) ^ half)
    dup = np.arange(SWA_KV_HEADS * 2 * SWA_HEAD_DIM)
    kv_dup = (dup // (2 * SWA_HEAD_DIM)) * SWA_HEAD_DIM + (dup % SWA_HEAD_DIM)
    kv_dup_swap = (kv_dup // SWA_HEAD_DIM) * SWA_HEAD_DIM + ((kv_dup % SWA_HEAD_DIM) ^ half)
    wq = w_in[:, :qw]
    wk = w_in[:, qw:qw + kvw]
    wv = w_in[:, qw + kvw:]
    w_cat = jnp.concatenate([wq, wk[:, kv_dup], wv[:, kv_dup]], axis=1)
    w_kv = jnp.concatenate([wk[:, kv_dup], wv[:, kv_dup]], axis=1)
    return w_cat.astype(BF16), w_kv.astype(BF16)


def kernel(x, c, ctx, c_ctx, ada_w, ada_b, norm1_g, norm2_g, gdn_w_in, gdn_conv_w, gdn_a_log, gdn_dt_bias,
           gdn_norm_g, gdn_w_out, swa_w_in, swa_sinks, swa_w_out, peer_w_query, peer_sub_keys, peer_u, peer_v,
           final_g):
    return _forward(x, c, ctx, c_ctx, ada_w, ada_b, norm1_g, norm2_g, gdn_w_in, gdn_conv_w, gdn_a_log,
                    gdn_dt_bias, gdn_norm_g, gdn_w_out, swa_w_in, swa_sinks, swa_w_out, peer_w_query,
                    peer_sub_keys, peer_u, peer_v, final_g)[0]


def _forward(x, c, ctx, c_ctx, ada_w, ada_b, norm1_g, norm2_g, gdn_w_in, gdn_conv_w, gdn_a_log, gdn_dt_bias,
             gdn_norm_g, gdn_w_out, swa_w_in, swa_sinks, swa_w_out, peer_w_query, peer_sub_keys, peer_u, peer_v,
             final_g):
    b, l, d = x.shape
    nctx = ctx.shape[1]
    depth = ada_w.shape[0]
    assert depth == 2 and l % 256 == 0 and nctx % 128 == 0 and b + 1 <= 16

    mod_rows = 16
    cmat = jnp.zeros((mod_rows, d), F32).at[:b].set(c).at[b].set(c_ctx)
    ada = _ada_params(cmat, ada_w, ada_b)
    mods = ada.reshape(depth, mod_rows, ADA_CHUNKS, d).transpose(0, 2, 1, 3)[:, :, :, None, :]

    xl = x.reshape(b * l, d)
    xc = ctx.reshape(b * nctx, d)

    def lat_seg(i, tm):
        return i // (l // tm)

    def ctx_seg(i, tm):
        return b

    streams = ((lat_seg, l), (ctx_seg, nctx))

    def peer_weights(li):
        wq_t = peer_w_query[li].T.astype(BF16)
        keys = peer_sub_keys[li].reshape(PEER_HEADS * 2, N_KEYS, -1).astype(BF16)
        u_p = peer_u[li].astype(BF16).reshape(-1, PEER_PIECE, d)
        vt_p = peer_v[li].astype(BF16).reshape(-1, PEER_PIECE, d).transpose(0, 2, 1)
        return wq_t, keys, u_p, vt_p

    final_row = final_g.reshape(1, d)

    sh1, sc1, g1, sh2, sc2, g2 = (mods[0, j] for j in range(ADA_CHUNKS))
    n1 = norm1_g[0].reshape(1, d)
    n2 = norm2_g[0].reshape(1, d)
    width = GDN_HEADS * GDN_DK
    w_in = gdn_w_in[0]
    w_main = w_in[:, :4 * width].astype(BF16)
    w_abt = w_in[:, 4 * width:].T
    conv_w8 = jnp.zeros((8, 3 * width), F32).at[:GDN_CONV].set(gdn_conv_w[0])
    alog_rows = jnp.broadcast_to(gdn_a_log[0].reshape(2 * GDN_HEADS, 1), (2 * GDN_HEADS, LANES))
    dt_rows = jnp.broadcast_to(gdn_dt_bias[0].reshape(2 * GDN_HEADS, 1), (2 * GDN_HEADS, LANES))
    gdn_ng = gdn_norm_g[0].reshape(1, GDN_DK)
    w_out0 = gdn_w_out[0].astype(BF16)

    tok = {}
    for name, xs, (seg, slen) in (("ctx", xc, streams[1]), ("lat", xl, streams[0])):
        tm = min(512, slen)
        p_main, ab_t = _gdn_inproj(xs, lambda i: seg(i, tm), tm, n1, sh1, sc1, w_main, w_abt)
        qh, kh, vh = _gdn_prep(p_main, b, slen, conv_w8)
        tok[name] = (xs, seg, slen, tm, p_main, ab_t, qh, kh, vh)

    s0 = jnp.zeros((b, 2, GDN_HEADS, GDN_DK, GDN_DK), F32)
    new_x = {}
    for name in ("ctx", "lat"):
        xs, seg, slen, tm, p_main, ab_t, qh, kh, vh = tok[name]
        o_f, o_b, s0 = _gdn_scan(qh, kh, vh, ab_t, b, slen, alog_rows, dt_rows, s0)
        new_x[name] = _gdn_out(o_f, o_b, p_main, xs, lambda i: seg(i, tm), tm, g1, gdn_ng, w_out0)

    wq_t, keys, u_bf, vt_bf = peer_weights(0)
    xl = _peer(new_x["lat"], lat_seg, l, n2, sh2, sc2, g2, wq_t, keys, u_bf, vt_bf, final_row, False)
    xc = _peer(new_x["ctx"], ctx_seg, b * nctx, n2, sh2, sc2, g2, wq_t, keys, u_bf, vt_bf, final_row, False)
    streams_seen = dict(x1=new_x["lat"], xc1=new_x["ctx"], x2=xl, xc2=xc)

    sh1, sc1, g1, sh2, sc2, g2 = (mods[1, j] for j in range(ADA_CHUNKS))
    n1 = norm1_g[1].reshape(1, d)
    n2 = norm2_g[1].reshape(1, d)
    w_cat, w_kv = _swa_weight_layout(swa_w_in[0])
    cos_t, sin_t = _rope_tables(l)
    tm = min(512, l)
    q, k2, v2 = _swa_inproj(xl, lambda i: lat_seg(i, tm), tm, n1, sh1, sc1, w_cat, cos_t, sin_t)
    tmc = min(512, nctx)
    k2c, v2c = _swa_ctx_kv(xc, lambda i: ctx_seg(i, tmc), tmc, n1, sh1, sc1, w_kv)
    sink_rows = jnp.broadcast_to(swa_sinks[0].reshape(SWA_Q_HEADS, 1), (SWA_Q_HEADS, LANES))
    att = _attention(q, k2, v2, k2c, v2c, sink_rows, b, l, nctx)
    xl = _proj_residual(att, xl, lambda i: lat_seg(i, tm), tm, g1, swa_w_out[0].astype(BF16))
    streams_seen["x3"] = xl
    wq_t, keys, u_bf, vt_bf = peer_weights(1)
    out = _peer(xl, lat_seg, l, n2, sh2, sc2, g2, wq_t, keys, u_bf, vt_bf, final_row, True)
    return out.reshape(b, l, d), streams_seen
```
